```python
import math
import jax, jax.numpy as jnp
from jax import lax
import numpy as np

D_MODEL = 4096
BATCH = 4
SEQ = 2048
DEPTH = 2
DEC_BATCH = 32
DEC_SEQ = 1
PAST_LEN = 16384
PAGE_SIZE = 128

HEAD_DIM = 64
ATT_WIDTH = D_MODEL // 2
N_Q_HEADS = ATT_WIDTH // HEAD_DIM
N_KV_HEADS = 8
Q_PER_KV = N_Q_HEADS // N_KV_HEADS
KV_WIDTH = N_KV_HEADS * HEAD_DIM
ATT_PROJ = ATT_WIDTH + 2 * KV_WIDTH
WINDOW = 128
BLOCK = 128
ROPE_DIM = HEAD_DIM // 4
ROPE_THETA = 500000.0
ATT_SCALE = HEAD_DIM ** -0.5
RWKV_WIDTH = D_MODEL - ATT_WIDTH
RWKV_HEAD = 64
N_RWKV_HEADS = RWKV_WIDTH // RWKV_HEAD
DECAY_LORA = 96
AAA_LORA = 96
MV_LORA = 64
GATE_LORA = 256
SHIFT_WIDTH = 3 * RWKV_WIDTH + DECAY_LORA + AAA_LORA + GATE_LORA
PROJ_WIDTH = ATT_PROJ + SHIFT_WIDTH
D_FF = 14336
N_EXPERTS = 8
TOP_K = 2
N_DENSE = (DEPTH + 1) // 2
N_MOE = DEPTH // 2
N_VRES = DEPTH - 1
RMS_EPS = 1e-5
GN_EPS = 64e-5

kernel_name = "hymba_rwkv7_swa_sink_moe_step"


def rms_norm(x, w):
    xf = x.astype(jnp.float32)
    y = xf * lax.rsqrt(jnp.mean(xf * xf, axis=-1, keepdims=True) + RMS_EPS)
    return (y * w.astype(jnp.float32)).astype(x.dtype)


def partial_rope(x, pos):
    half = ROPE_DIM // 2
    inv_freq = jnp.exp(-math.log(ROPE_THETA) * 2.0 * jnp.arange(half, dtype=jnp.float32) / ROPE_DIM)
    ang = pos.astype(jnp.float32)[:, None] * inv_freq[None, :]
    cos = jnp.cos(ang)[None, :, None, :]
    sin = jnp.sin(ang)[None, :, None, :]
    xr = x[..., :ROPE_DIM].astype(jnp.float32)
    x1, x2 = xr[..., :half], xr[..., half:]
    rot = jnp.concatenate([x1 * cos - x2 * sin, x2 * cos + x1 * sin], axis=-1)
    return jnp.concatenate([rot.astype(x.dtype), x[..., ROPE_DIM:]], axis=-1)


def attn_qkv(u_att, q_norm, k_norm, pos):
    B, T = u_att.shape[:2]
    q = u_att[..., :ATT_WIDTH].reshape(B, T, N_Q_HEADS, HEAD_DIM)
    k = u_att[..., ATT_WIDTH:ATT_WIDTH + KV_WIDTH].reshape(B, T, N_KV_HEADS, HEAD_DIM)
    v = u_att[..., ATT_WIDTH + KV_WIDTH:].reshape(B, T, N_KV_HEADS, HEAD_DIM)
    q = partial_rope(rms_norm(q, q_norm), pos)
    k = partial_rope(rms_norm(k, k_norm), pos)
    return q, k, v


def sink_attend(q, k, v, mask, sinks):
    s = jnp.einsum('bnqhgd,bnkhd->bnhgqk', q.astype(jnp.float32), k.astype(jnp.float32)) * ATT_SCALE
    s = jnp.where(mask[None, :, None, None], s, -jnp.inf)
    sink = sinks.astype(jnp.float32).reshape(N_KV_HEADS, Q_PER_KV)[None, None, :, :, None, None]
    m = jnp.maximum(jnp.max(s, axis=-1, keepdims=True), sink)
    p = jnp.exp(s - m)
    denom = jnp.sum(p, axis=-1, keepdims=True) + jnp.exp(sink - m)
    return jnp.einsum('bnhgqk,bnkhd->bnqhgd', p / denom, v.astype(jnp.float32))


def swa_prompt(q, k, v, sinks):
    B, S = q.shape[:2]
    nb = S // BLOCK
    qb = q.reshape(B, nb, BLOCK, N_KV_HEADS, Q_PER_KV, HEAD_DIM)
    kb = k.reshape(B, nb, BLOCK, N_KV_HEADS, HEAD_DIM)
    vb = v.reshape(B, nb, BLOCK, N_KV_HEADS, HEAD_DIM)

    def with_prev(t):
        prev = jnp.concatenate([jnp.zeros_like(t[:, :1]), t[:, :-1]], axis=1)
        return jnp.concatenate([prev, t], axis=2)

    qi = jnp.arange(BLOCK)[:, None]
    kj = jnp.arange(2 * BLOCK)[None, :]
    rel = qi - kj + BLOCK
    first = (jnp.arange(nb) == 0)[:, None, None]
    mask = (rel >= 0) & (rel <= WINDOW) & ~(first & (kj < BLOCK))
    o = sink_attend(qb, with_prev(kb), with_prev(vb), mask, sinks)
    return o.reshape(B, S, ATT_WIDTH).astype(q.dtype)


def swa_sample(q, k, v, buf_k, buf_v, sinks):
    Bd, T = q.shape[:2]
    W = buf_k.shape[1]
    kk = jnp.concatenate([buf_k.astype(k.dtype), k], axis=1)
    vv = jnp.concatenate([buf_v.astype(v.dtype), v], axis=1)
    qpos = jnp.arange(T) + W
    kpos = jnp.arange(W + T)
    rel = qpos[:, None] - kpos[None, :]
    mask = ((rel >= 0) & (rel <= WINDOW))[None]
    o = sink_attend(q.reshape(Bd, 1, T, N_KV_HEADS, Q_PER_KV, HEAD_DIM), kk[:, None], vv[:, None], mask, sinks)
    return o.reshape(Bd, T, ATT_WIDTH).astype(q.dtype), kk[:, T:], vv[:, T:]


def wkv_scan(r, w, k, v, a, b, S0):
    def step(S, inp):
        r_t, w_t, k_t, v_t, a_t, b_t = inp
        sa = jnp.einsum('bhij,bhj->bhi', S, a_t)
        S = S * w_t[:, :, None, :] + sa[..., None] * b_t[:, :, None, :] + v_t[..., None] * k_t[:, :, None, :]
        y = jnp.einsum('bhij,bhj->bhi', S, r_t)
        return S, y

    xs = tuple(jnp.moveaxis(t, 1, 0) for t in (r, w, k, v, a, b))
    S, ys = lax.scan(step, S0, xs)
    return jnp.moveaxis(ys, 0, 1), S


def rwkv_mix(u, prev_row, S0, v_first, lp):
    B, T = u.shape[:2]
    uf = u.astype(jnp.float32)
    prev = jnp.concatenate([prev_row[:, None].astype(jnp.float32), uf[:, :-1]], axis=1)
    xs = uf + (prev - uf) * lp['mu'].astype(jnp.float32)
    r = xs[..., :RWKV_WIDTH]
    k = xs[..., RWKV_WIDTH:2 * RWKV_WIDTH]
    v = xs[..., 2 * RWKV_WIDTH:3 * RWKV_WIDTH]
    o1 = 3 * RWKV_WIDTH
    wd = xs[..., o1:o1 + DECAY_LORA]
    ad = xs[..., o1 + DECAY_LORA:o1 + DECAY_LORA + AAA_LORA]
    gd = xs[..., o1 + DECAY_LORA + AAA_LORA:]
    w_log = -jax.nn.softplus(-(lp['w0'] + jnp.tanh(wd) @ lp['w2'])) - 0.5
    decay = jnp.exp(-jnp.exp(w_log))
    a = jax.nn.sigmoid(lp['a0'] + ad @ lp['a2'])
    g = jax.nn.sigmoid(gd) @ lp['g2']
    if v_first is None:
        v_first = v
    else:
        v = v + (v_first - v) * jax.nn.sigmoid(lp['v0'] + (v @ lp['v1']) @ lp['v2'])

    def heads(t):
        return t.reshape(B, T, N_RWKV_HEADS, RWKV_HEAD)

    kk = heads(k * lp['k_k'])
    kk = kk / jnp.maximum(jnp.sqrt(jnp.sum(kk * kk, axis=-1, keepdims=True)), 1e-12)
    k = k * (1.0 + (a - 1.0) * lp['k_a'])
    rh, kh, vh, ah = heads(r), heads(k), heads(v), heads(a)
    y, S = wkv_scan(rh, heads(decay), kh, vh, -kk, kk * ah, S0)
    mean = jnp.mean(y, axis=-1, keepdims=True)
    var = jnp.mean(jnp.square(y - mean), axis=-1, keepdims=True)
    y = ((y - mean) * lax.rsqrt(var + GN_EPS)).reshape(B, T, RWKV_WIDTH) * lp['ln_w'] + lp['ln_b']
    bonus = jnp.sum(rh * kh * lp['r_k'], axis=-1, keepdims=True) * vh
    out = (y + bonus.reshape(B, T, RWKV_WIDTH)) * g
    return out, uf[:, -1], S, v_first


def mixer_block(x, pos, lp, prev_row, S0, buf_k, buf_v, v_first, n_buf):
    h = rms_norm(x, lp['attn_norm'])
    u = h @ lp['w_in']
    q, k, v = attn_qkv(u[..., :ATT_PROJ], lp['q_norm'], lp['k_norm'], pos)
    if buf_k is None:
        att = swa_prompt(q, k, v, lp['sinks'])
        new_k, new_v = k[:, k.shape[1] - n_buf:], v[:, v.shape[1] - n_buf:]
    else:
        att, new_k, new_v = swa_sample(q, k, v, buf_k, buf_v, lp['sinks'])
    rw, new_row, S, v_first = rwkv_mix(u[..., ATT_PROJ:], prev_row, S0, v_first, lp)
    mix = jnp.concatenate([att.astype(x.dtype), rw.astype(x.dtype)], axis=-1) @ lp['w_out']
    return x + mix.astype(x.dtype), new_row, S, new_k, new_v, v_first


def swiglu(h, w1, w3, w2):
    return (jax.nn.silu(h @ w1) * (h @ w3)) @ w2


def moe_swiglu(h, router, w1, w3, w2):
    logits = (h @ router).astype(jnp.float32)
    top_v, top_i = lax.top_k(logits, TOP_K)
    gates = jax.nn.softmax(top_v, axis=-1)
    combine = jnp.sum(jax.nn.one_hot(top_i, N_EXPERTS, dtype=jnp.float32) * gates[..., None], axis=-2)
    out = jnp.zeros(h.shape, jnp.float32)
    for e in range(N_EXPERTS):
        out = out + combine[..., e:e + 1] * swiglu(h, w1[e], w3[e], w2[e]).astype(jnp.float32)
    return out.astype(h.dtype)


def channel_block(x, l, ffn_norm_w, dense_w1, dense_w3, dense_w2, router_w, moe_w1, moe_w3, moe_w2):
    h = rms_norm(x, ffn_norm_w[l])
    if l % 2 == 0:
        f = swiglu(h, dense_w1[l // 2], dense_w3[l // 2], dense_w2[l // 2])
    else:
        f = moe_swiglu(h, router_w[l // 2], moe_w1[l // 2], moe_w3[l // 2], moe_w2[l // 2])
    return x + f.astype(x.dtype)


def setup_inputs(seed: int = 0) -> dict:
    key = jax.random.key(seed)
    ks = jax.random.split(key, 34)
    f32 = jnp.float32
    n_buf = min(WINDOW, PAST_LEN)

    def nrm(i, shape, scale):
        return jax.random.normal(ks[i], shape, f32) * scale

    def gain(i, shape):
        return 1.0 + 0.01 * jax.random.normal(ks[i], shape, f32)

    return {
        'x_prompt': nrm(0, (BATCH, SEQ, D_MODEL), 1.0),
        'x_sample': nrm(1, (DEC_BATCH, DEC_SEQ, D_MODEL), 1.0),
        'state_shift': nrm(2, (DEPTH, DEC_BATCH, SHIFT_WIDTH), 1.0),
        'state_wkv': nrm(3, (DEPTH, DEC_BATCH, N_RWKV_HEADS, RWKV_HEAD, RWKV_HEAD), 0.3),
        'cache_win_k': nrm(4, (DEPTH, DEC_BATCH, n_buf, N_KV_HEADS, HEAD_DIM), 1.0),
        'cache_win_v': nrm(5, (DEPTH, DEC_BATCH, n_buf, N_KV_HEADS, HEAD_DIM), 1.0),
        'attn_norm_w': gain(6, (DEPTH, D_MODEL)),
        'w_in': nrm(7, (DEPTH, D_MODEL, PROJ_WIDTH), D_MODEL ** -0.5),
        'q_norm_w': gain(8, (DEPTH, HEAD_DIM)),
        'k_norm_w': gain(9, (DEPTH, HEAD_DIM)),
        'attn_sinks': nrm(10, (DEPTH, N_Q_HEADS), 1.0),
        'shift_mu': jax.random.uniform(ks[11], (DEPTH, SHIFT_WIDTH), f32),
        'w0': jax.random.uniform(ks[12], (DEPTH, RWKV_WIDTH), f32, -5.0, 0.0),
        'w2': nrm(13, (DEPTH, DECAY_LORA, RWKV_WIDTH), 0.5 * DECAY_LORA ** -0.5),
        'a0': nrm(14, (DEPTH, RWKV_WIDTH), 0.5),
        'a2': nrm(15, (DEPTH, AAA_LORA, RWKV_WIDTH), 0.5 * AAA_LORA ** -0.5),
        'g2': nrm(16, (DEPTH, GATE_LORA, RWKV_WIDTH), GATE_LORA ** -0.5),
        'k_k': 0.85 + 0.05 * jax.random.normal(ks[17], (DEPTH, RWKV_WIDTH), f32),
        'k_a': gain(18, (DEPTH, RWKV_WIDTH)),
        'r_k': nrm(19, (DEPTH, N_RWKV_HEADS, RWKV_HEAD), 0.1),
        'ln_x_w': gain(20, (DEPTH, RWKV_WIDTH)),
        'ln_x_b': nrm(21, (DEPTH, RWKV_WIDTH), 0.01),
        'v0': nrm(22, (N_VRES, RWKV_WIDTH), 0.5),
        'v1': nrm(23, (N_VRES, RWKV_WIDTH, MV_LORA), RWKV_WIDTH ** -0.5),
        'v2': nrm(24, (N_VRES, MV_LORA, RWKV_WIDTH), 0.5 * MV_LORA ** -0.5),
        'w_out': nrm(25, (DEPTH, D_MODEL, D_MODEL), D_MODEL ** -0.5),
        'ffn_norm_w': gain(26, (DEPTH, D_MODEL)),
        'dense_w1': nrm(27, (N_DENSE, D_MODEL, D_FF), D_MODEL ** -0.5),
        'dense_w3': nrm(28, (N_DENSE, D_MODEL, D_FF), D_MODEL ** -0.5),
        'dense_w2': nrm(29, (N_DENSE, D_FF, D_MODEL), D_FF ** -0.5),
        'router_w': nrm(30, (N_MOE, D_MODEL, N_EXPERTS), D_MODEL ** -0.5),
        'moe_w1': nrm(31, (N_MOE, N_EXPERTS, D_MODEL, D_FF), D_MODEL ** -0.5),
        'moe_w3': nrm(32, (N_MOE, N_EXPERTS, D_MODEL, D_FF), D_MODEL ** -0.5),
        'moe_w2': nrm(33, (N_MOE, N_EXPERTS, D_FF, D_MODEL), D_FF ** -0.5),
    }


def reference(x_prompt, x_sample, state_shift, state_wkv, cache_win_k, cache_win_v,
              attn_norm_w, w_in, q_norm_w, k_norm_w, attn_sinks, shift_mu, w0, w2, a0, a2, g2,
              k_k, k_a, r_k, ln_x_w, ln_x_b, v0, v1, v2, w_out, ffn_norm_w,
              dense_w1, dense_w3, dense_w2, router_w, moe_w1, moe_w3, moe_w2):
    n_buf = cache_win_k.shape[2]
    Bp = x_prompt.shape[0]
    pos_p = jnp.arange(x_prompt.shape[1], dtype=jnp.int32)
    pos_s = PAST_LEN + jnp.arange(x_sample.shape[1], dtype=jnp.int32)
    y_p, y_s = x_prompt, x_sample
    vf_p, vf_s = None, None
    p_rows, p_wkv, p_k, p_v = [], [], [], []
    s_rows, s_wkv, s_k, s_v = [], [], [], []
    for l in range(DEPTH):
        lp = {'attn_norm': attn_norm_w[l], 'w_in': w_in[l], 'q_norm': q_norm_w[l], 'k_norm': k_norm_w[l],
              'sinks': attn_sinks[l], 'mu': shift_mu[l], 'w0': w0[l], 'w2': w2[l], 'a0': a0[l], 'a2': a2[l],
              'g2': g2[l], 'k_k': k_k[l], 'k_a': k_a[l], 'r_k': r_k[l], 'ln_w': ln_x_w[l], 'ln_b': ln_x_b[l],
              'w_out': w_out[l]}
        if l > 0:
            lp['v0'] = v0[l - 1]
            lp['v1'] = v1[l - 1]
            lp['v2'] = v2[l - 1]
        y_p, row_p, S_p, k_p, v_p, vf_p = mixer_block(
            y_p, pos_p, lp, jnp.zeros((Bp, SHIFT_WIDTH), jnp.float32),
            jnp.zeros((Bp, N_RWKV_HEADS, RWKV_HEAD, RWKV_HEAD), jnp.float32), None, None, vf_p, n_buf)
        y_s, row_s, S_s, k_s, v_s, vf_s = mixer_block(
            y_s, pos_s, lp, state_shift[l], state_wkv[l].astype(jnp.float32),
            cache_win_k[l], cache_win_v[l], vf_s, n_buf)
        y_p = channel_block(y_p, l, ffn_norm_w, dense_w1, dense_w3, dense_w2, router_w, moe_w1, moe_w3, moe_w2)
        y_s = channel_block(y_s, l, ffn_norm_w, dense_w1, dense_w3, dense_w2, router_w, moe_w1, moe_w3, moe_w2)
        p_rows.append(row_p); p_wkv.append(S_p); p_k.append(k_p); p_v.append(v_p)
        s_rows.append(row_s); s_wkv.append(S_s); s_k.append(k_s); s_v.append(v_s)
    return (y_p, y_s,
            jnp.stack(p_rows).astype(state_shift.dtype), jnp.stack(p_wkv).astype(state_wkv.dtype),
            jnp.stack(p_k).astype(cache_win_k.dtype), jnp.stack(p_v).astype(cache_win_v.dtype),
            jnp.stack(s_rows).astype(state_shift.dtype), jnp.stack(s_wkv).astype(state_wkv.dtype),
            jnp.stack(s_k).astype(cache_win_k.dtype), jnp.stack(s_v).astype(cache_win_v.dtype))
```

```python
import functools
import math

import jax
import jax.numpy as jnp
from jax import lax
from jax.experimental import pallas as pl
from jax.experimental.pallas import tpu as pltpu

F32 = jnp.float32
BF16 = jnp.bfloat16
I32 = jnp.int32

D_MODEL = 4096
DEPTH = 2
PAST_LEN = 16384
HEAD_DIM = 64
ATT_WIDTH = D_MODEL // 2
N_Q_HEADS = ATT_WIDTH // HEAD_DIM
N_KV_HEADS = 8
Q_PER_KV = N_Q_HEADS // N_KV_HEADS
KV_WIDTH = N_KV_HEADS * HEAD_DIM
ATT_PROJ = ATT_WIDTH + 2 * KV_WIDTH
WINDOW = 128
BLOCK = 128
ROPE_DIM = HEAD_DIM // 4
ROPE_THETA = 500000.0
ATT_SCALE = HEAD_DIM ** -0.5
RWKV_WIDTH = D_MODEL - ATT_WIDTH
RWKV_HEAD = 64
N_RWKV_HEADS = RWKV_WIDTH // RWKV_HEAD
DECAY_LORA = 96
AAA_LORA = 96
GATE_LORA = 256
MV_LORA = 64
LORA_IN = DECAY_LORA + AAA_LORA + GATE_LORA
SHIFT_WIDTH = 3 * RWKV_WIDTH + LORA_IN
D_FF = 14336
N_EXPERTS = 8
TOP_K = 2
RMS_EPS = 1e-5
GN_EPS = 64e-5

VMEM_LIMIT_BYTES = 58 * 2 ** 20
LANES = 128
SUBLANES = 8
LORA_PAD = 4 * LANES
GROUP = 2 * LANES
N_GROUPS = RWKV_WIDTH // GROUP
HEADS_PER_GROUP = GROUP // RWKV_HEAD

MODE_SKIP, MODE_FULL, MODE_TAIL = 0, 1, 2


def _cdiv(a, b):
    return -(-a // b)


def _params(sem):
    return pltpu.CompilerParams(dimension_semantics=sem, vmem_limit_bytes=VMEM_LIMIT_BYTES)


def _split3(x):
    hi = x.astype(BF16)
    r1 = x - hi.astype(F32)
    mid = r1.astype(BF16)
    lo = (r1 - mid.astype(F32)).astype(BF16)
    return hi, mid, lo


def _dot_exact_rhs(x, rhs_bf16):
    hi, mid, lo = _split3(x)
    d = functools.partial(jnp.dot, preferred_element_type=F32)
    return d(hi, rhs_bf16) + d(mid, rhs_bf16) + d(lo, rhs_bf16)


def _rmsnorm_body(x_ref, w_ref, o_ref):
    x = x_ref[...]
    ms = jnp.mean(x * x, axis=-1, keepdims=True)
    o_ref[...] = (x * lax.rsqrt(ms + RMS_EPS) * w_ref[...]).astype(o_ref.dtype)


def rmsnorm_bf16(x, w, tr=512):
    M, D = x.shape
    return pl.pallas_call(
        _rmsnorm_body,
        grid=(_cdiv(M, tr),),
        in_specs=[pl.BlockSpec((tr, D), lambda i: (i, 0)), pl.BlockSpec((1, D), lambda i: (0, 0))],
        out_specs=pl.BlockSpec((tr, D), lambda i: (i, 0)),
        out_shape=jax.ShapeDtypeStruct((M, D), BF16),
        compiler_params=_params(("parallel",)),
        name="rmsnorm_bf16",
    )(x, w)


def _router_body(x_ref, w_ref, r_ref, idx_ref, gate_ref):
    x = x_ref[...]
    ms = jnp.mean(x * x, axis=-1, keepdims=True)
    h = x * lax.rsqrt(ms + RMS_EPS) * w_ref[...]
    logits = jnp.dot(h, r_ref[...], preferred_element_type=F32, precision=lax.Precision.HIGHEST)
    lane = lax.broadcasted_iota(I32, logits.shape, 1)
    neg = jnp.float32(-jnp.inf)
    l1 = jnp.where(lane < N_EXPERTS, logits, neg)
    m1 = jnp.max(l1, axis=-1, keepdims=True)
    i1 = jnp.min(jnp.where(l1 == m1, lane, LANES), axis=-1, keepdims=True)
    l2 = jnp.where(lane == i1, neg, l1)
    m2 = jnp.max(l2, axis=-1, keepdims=True)
    i2 = jnp.min(jnp.where(l2 == m2, lane, LANES), axis=-1, keepdims=True)
    e2 = jnp.exp(m2 - m1)
    den = 1.0 + e2
    idx_ref[...] = jnp.where(lane == 0, i1, jnp.where(lane == 1, i2, 0))
    gate_ref[...] = jnp.where(lane == 0, 1.0 / den, jnp.where(lane == 1, e2 / den, 0.0))


def route_top2(x, w, router_pad, tr=512):
    M, D = x.shape
    idx, gate = pl.pallas_call(
        _router_body,
        grid=(_cdiv(M, tr),),
        in_specs=[pl.BlockSpec((tr, D), lambda i: (i, 0)), pl.BlockSpec((1, D), lambda i: (0, 0)),
                  pl.BlockSpec((D, LANES), lambda i: (0, 0))],
        out_specs=[pl.BlockSpec((tr, LANES), lambda i: (i, 0)), pl.BlockSpec((tr, LANES), lambda i: (i, 0))],
        out_shape=[jax.ShapeDtypeStruct((M, LANES), I32), jax.ShapeDtypeStruct((M, LANES), F32)],
        compiler_params=_params(("parallel",)),
        name="route_top2",
    )(x, w, router_pad)
    return idx[:, :TOP_K], gate[:, :TOP_K]


def _first_of_weight(te_ref, m):
    prev = te_ref[jnp.maximum(m - 1, 0)]
    return jnp.logical_or(m == 0, te_ref[m] != prev)


def _for_rows(mode_ref, m, tail, fn, o_ref):
    pl.when(mode_ref[m] == MODE_FULL)(lambda: fn(slice(None)))
    if tail:
        pl.when(mode_ref[m] == MODE_TAIL)(lambda: fn(slice(0, tail)))
    pl.when(mode_ref[m] == MODE_SKIP)(lambda: o_ref.__setitem__(Ellipsis, jnp.zeros(o_ref.shape, o_ref.dtype)))


def _proj_body(te_ref, mode_ref, a_ref, w_ref, o_ref, wb_ref, *, tail):
    m = pl.program_id(1)
    pl.when(_first_of_weight(te_ref, m))(lambda: wb_ref.__setitem__(Ellipsis, w_ref[...].astype(BF16)))

    def rows(sl):
        o_ref[sl, :] = jnp.dot(a_ref[sl, :], wb_ref[...], preferred_element_type=F32)

    _for_rows(mode_ref, m, tail, rows, o_ref)


def matmul_proj(a, w, te, mode, *, n_lo, n_cols, tm, tn, tail=0):
    M, K = a.shape
    nb0 = n_lo // tn
    assert nb0 * tn == n_lo
    grid = (_cdiv(n_cols, tn), _cdiv(M, tm))
    return pl.pallas_call(
        functools.partial(_proj_body, tail=tail),
        grid_spec=pltpu.PrefetchScalarGridSpec(
            num_scalar_prefetch=2, grid=grid,
            in_specs=[pl.BlockSpec((tm, K), lambda n, m, te, md: (m, 0)),
                      pl.BlockSpec((None, K, tn), lambda n, m, te, md: (te[m], 0, nb0 + n))],
            out_specs=pl.BlockSpec((tm, tn), lambda n, m, te, md: (m, n)),
            scratch_shapes=[pltpu.VMEM((K, tn), BF16)]),
        out_shape=jax.ShapeDtypeStruct((M, n_cols), F32),
        compiler_params=_params(("arbitrary", "arbitrary")),
        name="matmul_proj",
    )(te, mode, a, w)


def _ffn1_body(te_ref, mode_ref, a_ref, w1_ref, w3_ref, o_ref, wb1_ref, wb3_ref, *, tail):
    m = pl.program_id(1)

    def cast():
        wb1_ref[...] = w1_ref[...].astype(BF16)
        wb3_ref[...] = w3_ref[...].astype(BF16)

    pl.when(_first_of_weight(te_ref, m))(cast)

    def rows(sl):
        a = a_ref[sl, :]
        u1 = jnp.dot(a, wb1_ref[...], preferred_element_type=F32)
        u3 = jnp.dot(a, wb3_ref[...], preferred_element_type=F32)
        o_ref[sl, :] = (u1 * jax.nn.sigmoid(u1) * u3).astype(o_ref.dtype)

    _for_rows(mode_ref, m, tail, rows, o_ref)


def matmul_swiglu_in(a, w1, w3, te, mode, *, tm, tn, tail=0):
    M, K = a.shape
    F = w1.shape[-1]
    grid = (_cdiv(F, tn), _cdiv(M, tm))
    wspec = pl.BlockSpec((None, K, tn), lambda n, m, te, md: (te[m], 0, n))
    return pl.pallas_call(
        functools.partial(_ffn1_body, tail=tail),
        grid_spec=pltpu.PrefetchScalarGridSpec(
            num_scalar_prefetch=2, grid=grid,
            in_specs=[pl.BlockSpec((tm, K), lambda n, m, te, md: (m, 0)), wspec, wspec],
            out_specs=pl.BlockSpec((tm, tn), lambda n, m, te, md: (m, n)),
            scratch_shapes=[pltpu.VMEM((K, tn), BF16), pltpu.VMEM((K, tn), BF16)]),
        out_shape=jax.ShapeDtypeStruct((M, F), BF16),
        compiler_params=_params(("arbitrary", "arbitrary")),
        name="matmul_swiglu_in",
    )(te, mode, a, w1, w3)


def _res_body(te_ref, mode_ref, a_ref, w_ref, res_ref, *rest, tail, scaled):
    if scaled:
        scale_ref, o_ref, wb_ref = rest
    else:
        o_ref, wb_ref = rest
    m = pl.program_id(1)
    pl.when(_first_of_weight(te_ref, m))(lambda: wb_ref.__setitem__(Ellipsis, w_ref[...].astype(BF16)))

    def rows(sl):
        d = jnp.dot(a_ref[sl, :], wb_ref[...], preferred_element_type=F32)
        if scaled:
            d = d * scale_ref[sl, :]
        o_ref[sl, :] = res_ref[sl, :] + d

    _for_rows(mode_ref, m, tail, rows, o_ref)


def matmul_residual(a, w, res, te, mode, *, tm, tn, tk, tail=0, scale=None):
    M, K = a.shape
    N = w.shape[-1]
    assert K % tk == 0 and N % tn == 0
    grid = (N // tn, _cdiv(M, tm))
    for kb in range(K // tk):
        in_specs = [pl.BlockSpec((tm, tk), lambda n, m, te, md, kb=kb: (m, kb)),
                    pl.BlockSpec((None, tk, tn), lambda n, m, te, md, kb=kb: (te[m], kb, n)),
                    pl.BlockSpec((tm, tn), lambda n, m, te, md: (m, n))]
        args = [te, mode, a, w, res]
        if scale is not None:
            in_specs.append(pl.BlockSpec((tm, 1), lambda n, m, te, md: (m, 0)))
            args.append(scale)
        res = pl.pallas_call(
            functools.partial(_res_body, tail=tail, scaled=scale is not None),
            grid_spec=pltpu.PrefetchScalarGridSpec(
                num_scalar_prefetch=2, grid=grid, in_specs=in_specs,
                out_specs=pl.BlockSpec((tm, tn), lambda n, m, te, md: (m, n)),
                scratch_shapes=[pltpu.VMEM((tk, tn), BF16)]),
            out_shape=jax.ShapeDtypeStruct((M, N), F32),
            input_output_aliases={4: 0},
            compiler_params=_params(("arbitrary", "arbitrary")),
            name="matmul_residual",
        )(*args)
    return res


def _dense_tiles(M, tm, e):
    nm = _cdiv(M, tm)
    tail = M - (nm - 1) * tm
    te = jnp.full((nm,), e, I32)
    if tail == tm:
        return te, jnp.full((nm,), MODE_FULL, I32), 0
    mode = jnp.array([MODE_FULL] * (nm - 1) + [MODE_TAIL], I32)
    return te, mode, tail


_hp_dot = functools.partial(jnp.dot, preferred_element_type=F32, precision=lax.Precision.HIGHEST)


def _small_body(*refs, kind, normed):
    it = iter(refs)
    x_ref = next(it)
    nw_ref = next(it) if normed else None
    a = x_ref[...]
    if normed:
        a = a * lax.rsqrt(jnp.mean(a * a, axis=-1, keepdims=True) + RMS_EPS) * nw_ref[...]
    if kind == 'swiglu':
        w1_ref, w3_ref, o_ref = it
        u1 = _hp_dot(a, w1_ref[...])
        u3 = _hp_dot(a, w3_ref[...])
        o_ref[...] = u1 * jax.nn.sigmoid(u1) * u3
    elif kind == 'res':
        w_ref, res_ref, o_ref = it
        o_ref[...] = res_ref[...] + _hp_dot(a, w_ref[...])
    else:
        w_ref, o_ref = it
        o_ref[...] = _hp_dot(a, w_ref[...])


def small_matmul(kind, x, ws, e, *, tn, norm_w=None, res=None, n_lo=0, n_cols=None):
    R, K = x.shape
    N = ws[0].shape[-1]
    n_cols = N - n_lo if n_cols is None else n_cols
    nb0 = n_lo // tn
    assert nb0 * tn == n_lo
    normed = norm_w is not None
    row = lambda w: pl.BlockSpec((R, w), lambda n: (0, 0))
    in_specs, args = [row(K)], [x]
    if normed:
        in_specs.append(pl.BlockSpec((1, K), lambda n: (0, 0)))
        args.append(norm_w)
    for w in ws:
        in_specs.append(pl.BlockSpec((None, K, tn), lambda n: (e, 0, nb0 + n)))
        args.append(w)
    ospec = pl.BlockSpec((R, tn), lambda n: (0, n))
    aliases = {}
    if kind == 'res':
        in_specs.append(ospec)
        aliases = {len(args): 0}
        args.append(res)
    return pl.pallas_call(
        functools.partial(_small_body, kind=kind, normed=normed),
        grid=(_cdiv(n_cols, tn),), in_specs=in_specs, out_specs=ospec,
        out_shape=jax.ShapeDtypeStruct((R, n_cols), F32),
        input_output_aliases=aliases,
        compiler_params=_params(("arbitrary",)),
        name="small_matmul_" + kind,
    )(*args)


def _gather_norm_body(src_ref, xp_hbm, xs_hbm, w_ref, o_ref, buf, sem, *, tg, mp):
    def issue(r, c):
        t = src_ref[r]
        dst = buf.at[pl.ds(r, 1), :]
        pl.when(t < mp)(lambda: pltpu.make_async_copy(xp_hbm.at[pl.ds(t, 1), :], dst, sem).start())
        pl.when(t >= mp)(lambda: pltpu.make_async_copy(xs_hbm.at[pl.ds(t - mp, 1), :], dst, sem).start())
        return c

    lax.fori_loop(0, tg, issue, 0)

    def wait(r, c):
        pltpu.make_async_copy(xp_hbm.at[pl.ds(0, 1), :], buf.at[pl.ds(r, 1), :], sem).wait()
        return c

    lax.fori_loop(0, tg, wait, 0)
    x = buf[...]
    ms = jnp.mean(x * x, axis=-1, keepdims=True)
    o_ref[...] = (x * lax.rsqrt(ms + RMS_EPS) * w_ref[...]).astype(o_ref.dtype)


def gather_rmsnorm_bf16(xp, xs, w, src, tg=256):
    Mp, D = xp.shape
    R = src.shape[0]
    assert R % tg == 0
    return pl.pallas_call(
        functools.partial(_gather_norm_body, tg=tg, mp=Mp),
        grid=(R // tg,),
        in_specs=[pl.BlockSpec((tg,), lambda i: (i,), memory_space=pltpu.SMEM),
                  pl.BlockSpec(memory_space=pl.ANY), pl.BlockSpec(memory_space=pl.ANY),
                  pl.BlockSpec((1, D), lambda i: (0, 0))],
        out_specs=pl.BlockSpec((tg, D), lambda i: (i, 0)),
        out_shape=jax.ShapeDtypeStruct((R, D), BF16),
        scratch_shapes=[pltpu.VMEM((tg, D), F32), pltpu.SemaphoreType.DMA(())],
        compiler_params=_params(("arbitrary",)),
        name="gather_rmsnorm_bf16",
    )(src, xp, xs, w)


def _combine_body(s0_ref, s1_ref, x_ref, y_hbm, o_ref, buf0, buf1, sem, *, tc):
    def issue(r, c):
        pltpu.make_async_copy(y_hbm.at[pl.ds(s0_ref[r], 1), :], buf0.at[pl.ds(r, 1), :], sem).start()
        pltpu.make_async_copy(y_hbm.at[pl.ds(s1_ref[r], 1), :], buf1.at[pl.ds(r, 1), :], sem).start()
        return c

    lax.fori_loop(0, tc, issue, 0)

    def wait(r, c):
        pltpu.make_async_copy(y_hbm.at[pl.ds(0, 1), :], buf0.at[pl.ds(r, 1), :], sem).wait()
        pltpu.make_async_copy(y_hbm.at[pl.ds(0, 1), :], buf1.at[pl.ds(r, 1), :], sem).wait()
        return c

    lax.fori_loop(0, tc, wait, 0)
    o_ref[...] = x_ref[...] + (buf0[...] + buf1[...])


def combine_top2(x, ys, slot0, slot1, tc=256):
    M, D = x.shape
    nt = _cdiv(M, tc)
    assert slot0.shape[0] == nt * tc
    sspec = pl.BlockSpec((tc,), lambda i: (i,), memory_space=pltpu.SMEM)
    return pl.pallas_call(
        functools.partial(_combine_body, tc=tc),
        grid=(nt,),
        in_specs=[sspec, sspec, pl.BlockSpec((tc, D), lambda i: (i, 0)), pl.BlockSpec(memory_space=pl.ANY)],
        out_specs=pl.BlockSpec((tc, D), lambda i: (i, 0)),
        out_shape=jax.ShapeDtypeStruct((M, D), F32),
        scratch_shapes=[pltpu.VMEM((tc, D), F32), pltpu.VMEM((tc, D), F32), pltpu.SemaphoreType.DMA(())],
        compiler_params=_params(("arbitrary",)),
        name="combine_top2",
    )(slot0, slot1, x, ys)


def _route_tables(idx, gates, tmg, n_tiles):
    M = idx.shape[0]
    e = idx.reshape(-1)
    onehot = (e[:, None] == jnp.arange(N_EXPERTS, dtype=I32)[None, :]).astype(I32)
    csum = jnp.cumsum(onehot, axis=0)
    rank = jnp.take_along_axis(csum, e[:, None], axis=1)[:, 0] - 1
    counts = csum[-1]
    tiles_e = (counts + tmg - 1) // tmg
    tile_end = jnp.cumsum(tiles_e)
    tile_start = tile_end - tiles_e
    slot = tile_start[e] * tmg + rank
    tiles = jnp.arange(n_tiles, dtype=I32)
    te_raw = jnp.sum((tiles[:, None] >= tile_end[None, :]).astype(I32), axis=1)
    total = tile_end[-1]
    valid = tiles < total
    te_last = jnp.sum((total - 1 >= tile_end).astype(I32))
    te = jnp.where(valid, te_raw, te_last).astype(I32)
    mode = jnp.where(valid, MODE_FULL, MODE_SKIP).astype(I32)
    rows = n_tiles * tmg
    src = jnp.zeros((rows,), I32).at[slot].set(jnp.arange(2 * M, dtype=I32) // 2)
    gate_s = jnp.zeros((rows,), F32).at[slot].set(gates.reshape(-1))
    return te, mode, src, gate_s[:, None], slot.reshape(M, TOP_K)


def _rope_tables(pos):
    half = ROPE_DIM // 2
    inv_freq = jnp.exp(-math.log(ROPE_THETA) * 2.0 * jnp.arange(half, dtype=F32) / ROPE_DIM)
    ang = pos.astype(F32)[:, None] * inv_freq[None, :]
    cos, sin = jnp.cos(ang), jnp.sin(ang)
    T = pos.shape[0]
    rest = HEAD_DIM - ROPE_DIM
    ctab = jnp.concatenate([cos, cos, jnp.ones((T, rest), F32)], axis=1)
    stab = jnp.concatenate([-sin, sin, jnp.zeros((T, rest), F32)], axis=1)
    d = jnp.arange(HEAD_DIM)
    partner = jnp.where(d < half, d + half, jnp.where(d < ROPE_DIM, d - half, d))
    swap = (d[:, None] == partner[None, :]).astype(BF16)
    return ctab, stab, swap


def _head_norm_rope(x, nw, ctab, stab, swap):
    y = x * lax.rsqrt(jnp.mean(x * x, axis=-1, keepdims=True) + RMS_EPS) * nw
    return y * ctab + _dot_exact_rhs(y, swap) * stab


def _attn_prompt_body(q_ref, kc_ref, kp_ref, vc_ref, vp_ref, cc_ref, sc_ref, cp_ref, sp_ref, qn_ref, kn_ref,
                      sink_ref, swap_ref, o_ref, knew_ref):
    i = pl.program_id(1)
    swap = swap_ref[...]
    cc, sc, cp, sp = cc_ref[...], sc_ref[...], cp_ref[...], sp_ref[...]
    qn, kn = qn_ref[...], kn_ref[...]
    R = Q_PER_KV * BLOCK
    row = lax.broadcasted_iota(I32, (R, 2 * BLOCK), 0)
    col = lax.broadcasted_iota(I32, (R, 2 * BLOCK), 1)
    rel = (row % BLOCK) - col + BLOCK
    mask = (rel >= 0) & (rel <= WINDOW) & jnp.logical_not((i == 0) & (col < BLOCK))
    rowh = lax.broadcasted_iota(I32, (R, 1), 0) // BLOCK
    for g in range(N_KV_HEADS):
        ks = slice(g * HEAD_DIM, (g + 1) * HEAD_DIM)
        k_cur = _head_norm_rope(kc_ref[:, ks], kn, cc, sc, swap)
        k_prev = _head_norm_rope(kp_ref[:, ks], kn, cp, sp, swap)
        knew_ref[:, ks] = k_cur
        kcat = jnp.concatenate([k_prev, k_cur], axis=0).astype(BF16)
        vcat = jnp.concatenate([vp_ref[:, ks], vc_ref[:, ks]], axis=0).astype(BF16)
        qs = []
        sink = jnp.zeros((R, 1), F32)
        for j in range(Q_PER_KV):
            h = g * Q_PER_KV + j
            qs.append(_head_norm_rope(q_ref[:, h * HEAD_DIM:(h + 1) * HEAD_DIM], qn, cc, sc, swap))
            sink = jnp.where(rowh == j, sink_ref[h], sink)
        qcat = jnp.concatenate(qs, axis=0).astype(BF16)
        s = lax.dot_general(qcat, kcat, (((1,), (1,)), ((), ())), preferred_element_type=F32) * ATT_SCALE
        s = jnp.where(mask, s, -jnp.inf)
        m = jnp.maximum(jnp.max(s, axis=-1, keepdims=True), sink)
        p = jnp.exp(s - m)
        denom = jnp.sum(p, axis=-1, keepdims=True) + jnp.exp(sink - m)
        o = jnp.dot((p / denom).astype(BF16), vcat, preferred_element_type=F32)
        for j in range(Q_PER_KV):
            h = g * Q_PER_KV + j
            o_ref[:, h * HEAD_DIM:(h + 1) * HEAD_DIM] = o[j * BLOCK:(j + 1) * BLOCK].astype(o_ref.dtype)


def attention_prompt(u_att, B, T, q_norm, k_norm, sinks):
    nb = T // BLOCK
    ctab, stab, swap = _rope_tables(jnp.arange(T, dtype=I32))
    kcol, vcol = ATT_WIDTH // KV_WIDTH, ATT_WIDTH // KV_WIDTH + 1
    cur = lambda b, i: b * nb + i
    prev = lambda b, i: b * nb + jnp.maximum(i - 1, 0)
    tspec_c = pl.BlockSpec((BLOCK, HEAD_DIM), lambda b, i: (i, 0))
    tspec_p = pl.BlockSpec((BLOCK, HEAD_DIM), lambda b, i: (jnp.maximum(i - 1, 0), 0))
    wspec = pl.BlockSpec((1, HEAD_DIM), lambda b, i: (0, 0))
    return pl.pallas_call(
        _attn_prompt_body,
        grid=(B, nb),
        in_specs=[pl.BlockSpec((BLOCK, ATT_WIDTH), lambda b, i: (cur(b, i), 0)),
                  pl.BlockSpec((BLOCK, KV_WIDTH), lambda b, i: (cur(b, i), kcol)),
                  pl.BlockSpec((BLOCK, KV_WIDTH), lambda b, i: (prev(b, i), kcol)),
                  pl.BlockSpec((BLOCK, KV_WIDTH), lambda b, i: (cur(b, i), vcol)),
                  pl.BlockSpec((BLOCK, KV_WIDTH), lambda b, i: (prev(b, i), vcol)),
                  tspec_c, tspec_c, tspec_p, tspec_p, wspec, wspec,
                  pl.BlockSpec(memory_space=pltpu.SMEM),
                  pl.BlockSpec((HEAD_DIM, HEAD_DIM), lambda b, i: (0, 0))],
        out_specs=[pl.BlockSpec((BLOCK, ATT_WIDTH), lambda b, i: (cur(b, i), 0)),
                   pl.BlockSpec((BLOCK, KV_WIDTH), lambda b, i: (b, 0))],
        out_shape=[jax.ShapeDtypeStruct((B * T, ATT_WIDTH), BF16),
                   jax.ShapeDtypeStruct((B * BLOCK, KV_WIDTH), F32)],
        compiler_params=_params(("arbitrary", "arbitrary")),
        name="attention_prompt",
    )(u_att, u_att, u_att, u_att, u_att, ctab, stab, ctab, stab, q_norm[None, :], k_norm[None, :], sinks, swap)


def _attn_sample_body(u_ref, k_ref, v_ref, ct_ref, st_ref, qn_ref, kn_ref, sink_ref, swap_ref, rexp_ref, bmask_ref,
                      o_ref, newk_ref, newv_ref, q_s, kx_s, vx_s, kn_s):
    b = pl.program_id(0)
    rexp = rexp_ref[...]

    @pl.when(b == 0)
    def _():
        swap = swap_ref[...]
        ct, st = ct_ref[...], st_ref[...]
        for h in range(N_Q_HEADS):
            sl = slice(h * HEAD_DIM, (h + 1) * HEAD_DIM)
            q_s[:, sl] = _head_norm_rope(u_ref[:, sl], qn_ref[...], ct, st, swap)
        for g in range(N_KV_HEADS):
            sl = slice(g * HEAD_DIM, (g + 1) * HEAD_DIM)
            kn_s[:, sl] = _head_norm_rope(u_ref[:, ATT_WIDTH + g * HEAD_DIM:ATT_WIDTH + (g + 1) * HEAD_DIM],
                                          kn_ref[...], ct, st, swap)
        kx_s[...] = _dot_exact_rhs(kn_s[...], rexp)
        vx_s[...] = _dot_exact_rhs(u_ref[:, ATT_WIDTH + KV_WIDTH:], rexp)

    bmask = bmask_ref[...]
    qfull = q_s[pl.ds(b, 1), :] * bmask
    kb, vb = k_ref[...], v_ref[...]
    kexp = _dot_exact_rhs(kb, rexp)
    vexp = _dot_exact_rhs(vb, rexp)
    s = lax.dot_general(qfull, kexp, (((1,), (1,)), ((), ())), preferred_element_type=F32,
                        precision=lax.Precision.HIGHEST) * ATT_SCALE
    s_self = jnp.sum(qfull * kx_s[pl.ds(b, 1), :], axis=-1, keepdims=True) * ATT_SCALE
    sink = sink_ref[...]
    m = jnp.maximum(jnp.maximum(jnp.max(s, axis=-1, keepdims=True), s_self), sink)
    p = jnp.exp(s - m)
    p_self = jnp.exp(s_self - m)
    denom = jnp.sum(p, axis=-1, keepdims=True) + p_self + jnp.exp(sink - m)
    o = _hp_dot(p / denom, vexp)
    o = o + (p_self / denom) * vx_s[pl.ds(b, 1), :]
    o_ref[pl.ds(b, 1), :] = jnp.sum(o * bmask, axis=0, keepdims=True)
    last = lax.broadcasted_iota(I32, kb.shape, 0) == WINDOW - 1
    newk_ref[...] = jnp.where(last, kn_s[pl.ds(b, 1), :], pltpu.roll(kb, WINDOW - 1, axis=0))
    newv_ref[...] = jnp.where(last, u_ref[pl.ds(b, 1), ATT_WIDTH + KV_WIDTH:], pltpu.roll(vb, WINDOW - 1, axis=0))


def attention_sample(u_att, row0, Bs, cache_k, cache_v, layer, q_norm, k_norm, sinks):
    assert row0 % Bs == 0
    ctab, stab, swap = _rope_tables(jnp.full((1,), PAST_LEN, I32))
    gd = jnp.arange(KV_WIDTH)
    hd = jnp.arange(ATT_WIDTH)
    rexp = ((gd[:, None] // HEAD_DIM == hd[None, :] // (HEAD_DIM * Q_PER_KV))
            & (gd[:, None] % HEAD_DIM == hd[None, :] % HEAD_DIM)).astype(BF16)
    bmask = (jnp.arange(N_Q_HEADS)[:, None] == hd[None, :] // HEAD_DIM).astype(F32)
    full = lambda shape: pl.BlockSpec(shape, lambda b: (0,) * len(shape))
    cspec = pl.BlockSpec((None, None, WINDOW, KV_WIDTH), lambda b: (layer, b, 0, 0))
    ospec = pl.BlockSpec((None, WINDOW, KV_WIDTH), lambda b: (b, 0, 0))
    return pl.pallas_call(
        _attn_sample_body,
        grid=(Bs,),
        in_specs=[pl.BlockSpec((Bs, ATT_PROJ), lambda b: (row0 // Bs, 0)), cspec, cspec,
                  full((1, HEAD_DIM)), full((1, HEAD_DIM)), full((1, HEAD_DIM)), full((1, HEAD_DIM)),
                  full((N_Q_HEADS, 1)), full((HEAD_DIM, HEAD_DIM)), full((KV_WIDTH, ATT_WIDTH)),
                  full((N_Q_HEADS, ATT_WIDTH))],
        out_specs=[full((Bs, ATT_WIDTH)), ospec, ospec],
        out_shape=[jax.ShapeDtypeStruct((Bs, ATT_WIDTH), F32),
                   jax.ShapeDtypeStruct((Bs, WINDOW, KV_WIDTH), F32),
                   jax.ShapeDtypeStruct((Bs, WINDOW, KV_WIDTH), F32)],
        scratch_shapes=[pltpu.VMEM((Bs, ATT_WIDTH), F32), pltpu.VMEM((Bs, ATT_WIDTH), F32),
                        pltpu.VMEM((Bs, ATT_WIDTH), F32), pltpu.VMEM((Bs, KV_WIDTH), F32)],
        compiler_params=_params(("arbitrary",)),
        name="attention_sample",
    )(u_att, cache_k, cache_v, ctab, stab, q_norm[None, :], k_norm[None, :], sinks[:, None], swap, rexp, bmask)


def _head_sum_matrix():
    i = jnp.arange(GROUP)
    return (i[:, None] // RWKV_HEAD == i[None, :] // RWKV_HEAD).astype(BF16)


def _head_sum(x, gmat):
    return jnp.concatenate([_dot_exact_rhs(x[:, q * GROUP:(q + 1) * GROUP], gmat) for q in range(N_GROUPS)], axis=1)


def _rwkv_prep_body(*refs, shifted, has_vres):
    it = iter(refs)
    u_ref = next(it)
    prev_ref = next(it)
    mu_ref, w0_ref, a0_ref, kk_ref, ka_ref, rk_ref, lora_ref, g_ref = (next(it) for _ in range(8))
    if has_vres:
        v0_ref, v1_ref, v2_ref, vf_ref = (next(it) for _ in range(4))
    r_o, w_o, k_o, v_o, a_o, b_o, g_o, bonus_o, tail_s = it
    u = u_ref[...]
    if shifted:
        first = jnp.where(pl.program_id(1) > 0, prev_ref[SUBLANES - 1:SUBLANES, :], 0.0)
        prev = jnp.where(lax.broadcasted_iota(I32, u.shape, 0) == 0, first, pltpu.roll(u, 1, axis=0))
    else:
        prev = prev_ref[...]
    xs = u + (prev - u) * mu_ref[...]
    r = xs[:, :RWKV_WIDTH]
    k = xs[:, RWKV_WIDTH:2 * RWKV_WIDTH]
    v = xs[:, 2 * RWKV_WIDTH:3 * RWKV_WIDTH]
    tail_s[...] = jnp.zeros(tail_s.shape, F32)
    tail_s[:, :LORA_IN] = xs[:, 3 * RWKV_WIDTH:]
    t = tail_s[...]
    def dot(x, w):
        if w.dtype == F32:
            return _hp_dot(x, w)
        return jnp.dot(x.astype(BF16), w, preferred_element_type=F32)

    dw = dot(jnp.tanh(t), lora_ref[0])
    da = dot(t, lora_ref[1])
    g = dot(jax.nn.sigmoid(t), lora_ref[2])
    z = -(w0_ref[...] + dw)
    softplus = jnp.maximum(z, 0.0) + jnp.log(1.0 + jnp.exp(-jnp.abs(z)))
    w_log = -softplus - 0.5
    decay = jnp.exp(-jnp.exp(w_log))
    a = jax.nn.sigmoid(a0_ref[...] + da)
    if has_vres:
        lo = dot(dot(v, v1_ref[...]), v2_ref[...])
        v = v + (vf_ref[...] - v) * jax.nn.sigmoid(v0_ref[...] + lo)
    gmat = g_ref[...]
    kk = k * kk_ref[...]
    kk = kk / jnp.maximum(jnp.sqrt(_head_sum(kk * kk, gmat)), 1e-12)
    k = k * (1.0 + (a - 1.0) * ka_ref[...])
    r_o[...] = r
    w_o[...] = decay
    k_o[...] = k
    v_o[...] = v
    a_o[...] = -kk
    b_o[...] = kk * a
    g_o[...] = g
    bonus_o[...] = _head_sum(r * k * rk_ref[...], gmat) * v


def rwkv_prep(u_rw, row0, B, T, prev_rows, p, v_first, tr):
    n = B * T
    shifted = prev_rows is None
    if shifted:
        assert T % tr == 0 and row0 == 0
        nt = T // tr
        grid = (B, nt)
        rowblk = lambda b, i: b * nt + i
        prev_spec = pl.BlockSpec((SUBLANES, SHIFT_WIDTH),
                                 lambda b, i: (jnp.maximum(rowblk(b, i) * (tr // SUBLANES) - 1, 0), 0))
        prev_arr = u_rw
        oblk = lambda b, i: (b * nt + i, 0)
    else:
        assert T == 1 and tr == B and row0 % tr == 0
        grid = (1, 1)
        rowblk = lambda b, i: row0 // tr
        prev_spec = pl.BlockSpec((tr, SHIFT_WIDTH), lambda b, i: (0, 0))
        prev_arr = prev_rows
        oblk = lambda b, i: (0, 0)
    vec = lambda w: pl.BlockSpec((1, w), lambda b, i: (0, 0))
    in_specs = [pl.BlockSpec((tr, SHIFT_WIDTH), lambda b, i: (rowblk(b, i), 0)), prev_spec,
                vec(SHIFT_WIDTH), vec(RWKV_WIDTH), vec(RWKV_WIDTH), vec(RWKV_WIDTH), vec(RWKV_WIDTH),
                vec(RWKV_WIDTH), pl.BlockSpec((3, LORA_PAD, RWKV_WIDTH), lambda b, i: (0, 0, 0)),
                pl.BlockSpec((GROUP, GROUP), lambda b, i: (0, 0))]
    args = [u_rw, prev_arr, p['mu'], p['w0'], p['a0'], p['k_k'], p['k_a'], p['r_k'], p['lora'], _head_sum_matrix()]
    has_vres = v_first is not None
    if has_vres:
        in_specs += [vec(RWKV_WIDTH), pl.BlockSpec((RWKV_WIDTH, MV_LORA), lambda b, i: (0, 0)),
                     pl.BlockSpec((MV_LORA, RWKV_WIDTH), lambda b, i: (0, 0)),
                     pl.BlockSpec((tr, RWKV_WIDTH), oblk)]
        args += [p['v0'], p['v1'], p['v2'], v_first]
    ospec = pl.BlockSpec((tr, RWKV_WIDTH), oblk)
    return pl.pallas_call(
        functools.partial(_rwkv_prep_body, shifted=shifted, has_vres=has_vres),
        grid=grid, in_specs=in_specs, out_specs=[ospec] * 8,
        out_shape=[jax.ShapeDtypeStruct((n, RWKV_WIDTH), F32)] * 8,
        scratch_shapes=[pltpu.VMEM((tr, LORA_PAD), F32)],
        compiler_params=_params(("arbitrary", "arbitrary")),
        name="rwkv_prep",
    )(*args)


def _rwkv_params(l, shift_mu, w0, w2, a0, a2, g2, k_k, k_a, r_k, v0, v1, v2):
    lora = jnp.zeros((3, LORA_PAD, RWKV_WIDTH), F32)
    lora = lora.at[0, :DECAY_LORA].set(w2[l])
    lora = lora.at[1, DECAY_LORA:DECAY_LORA + AAA_LORA].set(a2[l])
    lora = lora.at[2, DECAY_LORA + AAA_LORA:LORA_IN].set(g2[l])
    p = {'mu': shift_mu[l][None, :], 'w0': w0[l][None, :], 'a0': a0[l][None, :], 'k_k': k_k[l][None, :],
         'k_a': k_a[l][None, :], 'r_k': r_k[l].reshape(1, RWKV_WIDTH), 'lora': lora}
    if l > 0:
        p['v0'] = v0[l - 1][None, :]
        p['v1'] = v1[l - 1]
        p['v2'] = v2[l - 1]
    p_bf16 = {k_: (v_.astype(BF16) if k_ in ('lora', 'v1', 'v2') else v_) for k_, v_ in p.items()}
    return p, p_bf16


def _wkv_body(r_ref, w_ref, k_ref, v_ref, a_ref, b_ref, s0_ref, g_ref, eye_ref, y_ref, sout_ref, s_s, *, nb, tc,
              precise):
    c = pl.program_id(1)

    @pl.when(c == 0)
    def _():
        s_s[...] = s0_ref[...]

    gmat = g_ref[...]
    eye = eye_ref[...][None]
    rows = nb * RWKV_HEAD
    dot = functools.partial(jnp.dot, preferred_element_type=F32)

    def pieces(x):
        x = x.reshape(rows, GROUP)
        return list(_split3(x)) if precise else [x.astype(BF16)]

    npc = 3 if precise else 1

    def head_sums(lhs, n):
        red = dot(jnp.concatenate(lhs, axis=0), gmat)
        out = []
        for q in range(n):
            o = q * npc * rows
            acc = red[o:o + rows]
            for j in range(1, npc):
                acc = acc + red[o + j * rows:o + (j + 1) * rows]
            out.append(acc.reshape(nb, RWKV_HEAD, GROUP))
        return out

    def step(t, carry):
        r_t, w_t, k_t, v_t, a_t, b_t = (ref[:, pl.ds(t, 1), :] for ref in (r_ref, w_ref, k_ref, v_ref, a_ref, b_ref))
        lhs = []
        for q in range(N_GROUPS):
            sl = slice(q * GROUP, (q + 1) * GROUP)
            lhs += pieces(s_s[q] * a_t[:, :, sl])
        sa = head_sums(lhs, N_GROUPS)
        lhs = []
        for q in range(N_GROUPS):
            sl = slice(q * GROUP, (q + 1) * GROUP)
            v_hi = v_t[:, :, sl].astype(BF16).astype(F32)
            v_mid = (v_t[:, :, sl] - v_hi).astype(BF16).astype(F32)
            vp = [v_hi, v_mid, v_t[:, :, sl] - v_hi - v_mid] if precise else [v_hi, v_t[:, :, sl] - v_hi]
            lhs += [(eye * x).reshape(rows, GROUP).astype(BF16) for x in vp]
        red = dot(jnp.concatenate(lhs, axis=0), gmat)
        nv = 3 if precise else 2
        lhs2 = []
        for q in range(N_GROUPS):
            sl = slice(q * GROUP, (q + 1) * GROUP)
            o = q * nv * rows
            vx = red[o:o + rows]
            for j in range(1, nv):
                vx = vx + red[o + j * rows:o + (j + 1) * rows]
            vx = vx.reshape(nb, RWKV_HEAD, GROUP)
            s_new = s_s[q] * w_t[:, :, sl] + sa[q] * b_t[:, :, sl] + vx * k_t[:, :, sl]
            s_s[q] = s_new
            lhs2 += pieces(s_new * r_t[:, :, sl])
        yx = head_sums(lhs2, N_GROUPS)
        for q in range(N_GROUPS):
            y_ref[:, pl.ds(t, 1), q * GROUP:(q + 1) * GROUP] = jnp.sum(yx[q] * eye, axis=1, keepdims=True)
        return carry

    lax.fori_loop(0, tc, step, 0)

    @pl.when(c == pl.num_programs(1) - 1)
    def _():
        sout_ref[...] = s_s[...]


def wkv_scan(r, w, k, v, a, b, s0, nb=4, tc=64, precise=False):
    B, T, W = r.shape
    tc = min(tc, T)
    assert B % nb == 0 and T % tc == 0
    xspec = pl.BlockSpec((nb, tc, W), lambda i, c: (i, c, 0))
    sspec = pl.BlockSpec((N_GROUPS, nb, RWKV_HEAD, GROUP), lambda i, c: (0, i, 0, 0))
    j = jnp.arange(GROUP)
    eye = (jnp.arange(RWKV_HEAD)[:, None] == j[None, :] % RWKV_HEAD).astype(F32)
    return pl.pallas_call(
        functools.partial(_wkv_body, nb=nb, tc=tc, precise=precise),
        grid=(B // nb, T // tc),
        in_specs=[xspec] * 6 + [sspec, pl.BlockSpec((GROUP, GROUP), lambda i, c: (0, 0)),
                                pl.BlockSpec((RWKV_HEAD, GROUP), lambda i, c: (0, 0))],
        out_specs=[xspec, sspec],
        out_shape=[jax.ShapeDtypeStruct((B, T, W), F32), jax.ShapeDtypeStruct(s0.shape, F32)],
        scratch_shapes=[pltpu.VMEM((N_GROUPS, nb, RWKV_HEAD, GROUP), F32)],
        compiler_params=_params(("arbitrary", "arbitrary")),
        name="wkv_scan",
    )(r, w, k, v, a, b, s0, _head_sum_matrix(), eye)


def _state_to_groups(s):
    B = s.shape[0]
    return s.reshape(B, N_GROUPS, HEADS_PER_GROUP, RWKV_HEAD, RWKV_HEAD).transpose(1, 0, 3, 2, 4).reshape(
        N_GROUPS, B, RWKV_HEAD, GROUP)


def _state_from_groups(s):
    B = s.shape[1]
    return s.reshape(N_GROUPS, B, RWKV_HEAD, HEADS_PER_GROUP, RWKV_HEAD).transpose(1, 0, 3, 2, 4).reshape(
        B, N_RWKV_HEADS, RWKV_HEAD, RWKV_HEAD)


def _rwkv_post_body(y_ref, bonus_ref, g_ref, lw_ref, lb_ref, gm_ref, o_ref):
    gmat = gm_ref[...]
    y = y_ref[...]
    mean = _head_sum(y, gmat) * (1.0 / RWKV_HEAD)
    d = y - mean
    var = _head_sum(d * d, gmat) * (1.0 / RWKV_HEAD)
    yn = d * lax.rsqrt(var + GN_EPS) * lw_ref[...] + lb_ref[...]
    o_ref[...] = ((yn + bonus_ref[...]) * g_ref[...]).astype(o_ref.dtype)


def rwkv_post(y, bonus, g, ln_w, ln_b, out_dtype, tr=256):
    n, W = y.shape
    tr = min(tr, n)
    assert n % tr == 0
    spec = pl.BlockSpec((tr, W), lambda i: (i, 0))
    vec = pl.BlockSpec((1, W), lambda i: (0, 0))
    return pl.pallas_call(
        _rwkv_post_body, grid=(n // tr,),
        in_specs=[spec, spec, spec, vec, vec, pl.BlockSpec((GROUP, GROUP), lambda i: (0, 0))],
        out_specs=spec, out_shape=jax.ShapeDtypeStruct((n, W), out_dtype),
        compiler_params=_params(("parallel",)),
        name="rwkv_post",
    )(y, bonus, g, ln_w[None, :], ln_b[None, :], _head_sum_matrix())


def _rwkv_mixer(u_rw, B, T, prev_rows, s0_groups, p, v_first, ln_w, ln_b, tr, precise):
    r, w, k, v, a, b, g, bonus = rwkv_prep(u_rw, 0, B, T, prev_rows, p, v_first, tr)
    sh = lambda t: t.reshape(B, T, RWKV_WIDTH)
    y, s_out = wkv_scan(sh(r), sh(w), sh(k), sh(v), sh(a), sh(b), s0_groups, precise=precise)
    out = rwkv_post(y.reshape(B * T, RWKV_WIDTH), bonus, g, ln_w, ln_b, F32 if precise else BF16)
    return out, _state_from_groups(s_out), v


TM = 1024
TM_MOE = 256
TC = 256
TR_PREP = 128
TN_SMALL = 512


def kernel(x_prompt, x_sample, state_shift, state_wkv, cache_win_k, cache_win_v, attn_norm_w, w_in, q_norm_w,
           k_norm_w, attn_sinks, shift_mu, w0, w2, a0, a2, g2, k_k, k_a, r_k, ln_x_w, ln_x_b, v0, v1, v2, w_out,
           ffn_norm_w, dense_w1, dense_w3, dense_w2, router_w, moe_w1, moe_w3, moe_w2):
    Bp, Tp, D = x_prompt.shape
    Bs, Ts, _ = x_sample.shape
    assert Ts == 1 and cache_win_k.shape[2] == WINDOW and Tp % BLOCK == 0
    Mp, Ms = Bp * Tp, Bs * Ts
    assert Mp % TM == 0 and Mp % TC == 0
    xp = x_prompt.reshape(Mp, D)
    xs = x_sample.reshape(Ms, D)
    cache_k = cache_win_k.reshape(DEPTH, Bs, WINDOW, KV_WIDTH)
    cache_v = cache_win_v.reshape(DEPTH, Bs, WINDOW, KV_WIDTH)
    n_moe_tiles = (TOP_K * (Mp + Ms)) // TM_MOE + N_EXPERTS
    zero_state = jnp.zeros((N_GROUPS, Bp, RWKV_HEAD, GROUP), F32)
    vf_p = vf_s = None
    outs = {k_: [] for k_ in ('p_row', 'p_wkv', 'p_k', 'p_v', 's_row', 's_wkv', 's_k', 's_v')}
    for l in range(DEPTH):
        te, mode, _ = _dense_tiles(Mp, TM, l)
        p_f32, p_bf16 = _rwkv_params(l, shift_mu, w0, w2, a0, a2, g2, k_k, k_a, r_k, v0, v1, v2)
        an = attn_norm_w[l][None, :]
        h = rmsnorm_bf16(xp, an)
        u_att = matmul_proj(h, w_in, te, mode, n_lo=0, n_cols=ATT_PROJ, tm=TM, tn=512)
        u_rw = matmul_proj(h, w_in, te, mode, n_lo=ATT_PROJ, n_cols=SHIFT_WIDTH, tm=TM, tn=512)
        att_p, knew_p = attention_prompt(u_att, Bp, Tp, q_norm_w[l], k_norm_w[l], attn_sinks[l])
        rw_p, S_p, v_p_first = _rwkv_mixer(u_rw, Bp, Tp, None, zero_state, p_bf16, vf_p, ln_x_w[l], ln_x_b[l],
                                           TR_PREP, False)
        xp = matmul_residual(jnp.concatenate([att_p, rw_p], axis=1), w_out, xp, te, mode, tm=TM, tn=512, tk=D)
        us_att = small_matmul('proj', xs, [w_in], l, tn=TN_SMALL, norm_w=an, n_lo=0, n_cols=ATT_PROJ)
        us_rw = small_matmul('proj', xs, [w_in], l, tn=TN_SMALL, norm_w=an, n_lo=ATT_PROJ, n_cols=SHIFT_WIDTH)
        att_s, k_s, v_s = attention_sample(us_att, 0, Bs, cache_k, cache_v, l, q_norm_w[l], k_norm_w[l],
                                           attn_sinks[l])
        rw_s, S_s, v_s_first = _rwkv_mixer(us_rw, Bs, Ts, state_shift[l], _state_to_groups(state_wkv[l]), p_f32,
                                           vf_s, ln_x_w[l], ln_x_b[l], Bs, True)
        xs = small_matmul('res', jnp.concatenate([att_s, rw_s], axis=1), [w_out], l, tn=TN_SMALL, res=xs)
        if l == 0:
            vf_p, vf_s = v_p_first, v_s_first
        fn = ffn_norm_w[l][None, :]
        if l % 2 == 0:
            te, mode, _ = _dense_tiles(Mp, TM, l // 2)
            h = rmsnorm_bf16(xp, fn)
            g = matmul_swiglu_in(h, dense_w1, dense_w3, te, mode, tm=TM, tn=256)
            xp = matmul_residual(g, dense_w2, xp, te, mode, tm=TM, tn=512, tk=D_FF // 4)
            gs = small_matmul('swiglu', xs, [dense_w1, dense_w3], l // 2, tn=TN_SMALL // 2, norm_w=fn)
            xs = small_matmul('res', gs, [dense_w2], l // 2, tn=TN_SMALL // 4, res=xs)
        else:
            router_pad = jnp.pad(router_w[l // 2], ((0, 0), (0, LANES - N_EXPERTS)))
            idx_p, gates_p = route_top2(xp, fn, router_pad)
            idx_s, gates_s = route_top2(xs, fn, router_pad, tr=Ms)
            te, mode, src, gate_rows, slot = _route_tables(
                jnp.concatenate([idx_p, idx_s], axis=0), jnp.concatenate([gates_p, gates_s], axis=0),
                TM_MOE, n_moe_tiles)
            rows = gather_rmsnorm_bf16(xp, xs, fn, src, tg=TC)
            g = matmul_swiglu_in(rows, moe_w1[l // 2], moe_w3[l // 2], te, mode, tm=TM_MOE, tn=512)
            ys = matmul_residual(g, moe_w2[l // 2], jnp.zeros((rows.shape[0], D), F32), te, mode,
                                 tm=TM_MOE, tn=1024, tk=D_FF // 4, scale=gate_rows)
            xp = combine_top2(xp, ys, slot[:Mp, 0], slot[:Mp, 1], tc=TC)
            xs = combine_top2(xs, ys, slot[Mp:, 0], slot[Mp:, 1], tc=Ms)
        u_att_p = u_att.reshape(Bp, Tp, ATT_PROJ)
        outs['p_row'].append(u_rw.reshape(Bp, Tp, SHIFT_WIDTH)[:, Tp - 1])
        outs['p_wkv'].append(S_p)
        outs['p_k'].append(knew_p.reshape(Bp, WINDOW, N_KV_HEADS, HEAD_DIM))
        outs['p_v'].append(u_att_p[:, Tp - WINDOW:, ATT_WIDTH + KV_WIDTH:].reshape(Bp, WINDOW, N_KV_HEADS, HEAD_DIM))
        outs['s_row'].append(us_rw)
        outs['s_wkv'].append(S_s)
        outs['s_k'].append(k_s.reshape(Bs, WINDOW, N_KV_HEADS, HEAD_DIM))
        outs['s_v'].append(v_s.reshape(Bs, WINDOW, N_KV_HEADS, HEAD_DIM))
    return (xp.reshape(Bp, Tp, D), xs.reshape(Bs, Ts, D),
            jnp.stack(outs['p_row']), jnp.stack(outs['p_wkv']), jnp.stack(outs['p_k']), jnp.stack(outs['p_v']),
            jnp.stack(outs['s_row']), jnp.stack(outs['s_wkv']), jnp.stack(outs['s_k']), jnp.stack(outs['s_v']))
```

```python
import functools
import math

import jax
import jax.numpy as jnp
from jax import lax
from jax.experimental import pallas as pl
from jax.experimental.pallas import tpu as pltpu

F32 = jnp.float32
BF16 = jnp.bfloat16
I32 = jnp.int32

D_MODEL = 4096
DEPTH = 2
PAST_LEN = 16384
HEAD_DIM = 64
ATT_WIDTH = D_MODEL // 2
N_Q_HEADS = ATT_WIDTH // HEAD_DIM
N_KV_HEADS = 8
Q_PER_KV = N_Q_HEADS // N_KV_HEADS
KV_WIDTH = N_KV_HEADS * HEAD_DIM
ATT_PROJ = ATT_WIDTH + 2 * KV_WIDTH
WINDOW = 128
BLOCK = 128
ROPE_DIM = HEAD_DIM // 4
ROPE_THETA = 500000.0
ATT_SCALE = HEAD_DIM ** -0.5
RWKV_WIDTH = D_MODEL - ATT_WIDTH
RWKV_HEAD = 64
N_RWKV_HEADS = RWKV_WIDTH // RWKV_HEAD
DECAY_LORA = 96
AAA_LORA = 96
GATE_LORA = 256
MV_LORA = 64
LORA_IN = DECAY_LORA + AAA_LORA + GATE_LORA
SHIFT_WIDTH = 3 * RWKV_WIDTH + LORA_IN
D_FF = 14336
N_EXPERTS = 8
TOP_K = 2
RMS_EPS = 1e-5
GN_EPS = 64e-5

VMEM_LIMIT_BYTES = 58 * 2 ** 20
LANES = 128
SUBLANES = 8
BF16_ROWS = 2 * SUBLANES
LORA_PAD = 4 * LANES
GROUP = 2 * LANES
N_GROUPS = RWKV_WIDTH // GROUP
HEADS_PER_GROUP = GROUP // RWKV_HEAD

MODE_SKIP, MODE_FULL, MODE_TAIL = 0, 1, 2
DMA_UNROLL = 8


def _cdiv(a, b):
    return -(-a // b)


def _params(sem):
    return pltpu.CompilerParams(dimension_semantics=sem, vmem_limit_bytes=VMEM_LIMIT_BYTES)


def _split3(x):
    hi = x.astype(BF16)
    r1 = x - hi.astype(F32)
    mid = r1.astype(BF16)
    lo = (r1 - mid.astype(F32)).astype(BF16)
    return hi, mid, lo


def _dot_exact_rhs(x, rhs_bf16):
    hi, mid, lo = _split3(x)
    d = functools.partial(jnp.dot, preferred_element_type=F32)
    return d(hi, rhs_bf16) + d(mid, rhs_bf16) + d(lo, rhs_bf16)


def _rmsnorm_body(x_ref, w_ref, o_ref):
    x = x_ref[...]
    ms = jnp.mean(x * x, axis=-1, keepdims=True)
    o_ref[...] = (x * lax.rsqrt(ms + RMS_EPS) * w_ref[...]).astype(o_ref.dtype)


def rmsnorm_bf16(x, w, tr=512):
    M, D = x.shape
    return pl.pallas_call(
        _rmsnorm_body,
        grid=(_cdiv(M, tr),),
        in_specs=[pl.BlockSpec((tr, D), lambda i: (i, 0)), pl.BlockSpec((1, D), lambda i: (0, 0))],
        out_specs=pl.BlockSpec((tr, D), lambda i: (i, 0)),
        out_shape=jax.ShapeDtypeStruct((M, D), BF16),
        compiler_params=_params(("parallel",)),
        name="rmsnorm_bf16",
    )(x, w)


def _router_body(x_ref, w_ref, r_ref, idx_ref, gate_ref):
    x = x_ref[...]
    ms = jnp.mean(x * x, axis=-1, keepdims=True)
    h = x * lax.rsqrt(ms + RMS_EPS) * w_ref[...]
    logits = jnp.dot(h, r_ref[...], preferred_element_type=F32, precision=lax.Precision.HIGHEST)
    lane = lax.broadcasted_iota(I32, logits.shape, 1)
    neg = jnp.float32(-jnp.inf)
    l1 = jnp.where(lane < N_EXPERTS, logits, neg)
    m1 = jnp.max(l1, axis=-1, keepdims=True)
    i1 = jnp.min(jnp.where(l1 == m1, lane, LANES), axis=-1, keepdims=True)
    l2 = jnp.where(lane == i1, neg, l1)
    m2 = jnp.max(l2, axis=-1, keepdims=True)
    i2 = jnp.min(jnp.where(l2 == m2, lane, LANES), axis=-1, keepdims=True)
    e2 = jnp.exp(m2 - m1)
    den = 1.0 + e2
    idx_ref[...] = jnp.where(lane == 0, i1, jnp.where(lane == 1, i2, 0))
    gate_ref[...] = jnp.where(lane == 0, 1.0 / den, jnp.where(lane == 1, e2 / den, 0.0))


def route_top2(x, w, router_pad, tr=512):
    M, D = x.shape
    idx, gate = pl.pallas_call(
        _router_body,
        grid=(_cdiv(M, tr),),
        in_specs=[pl.BlockSpec((tr, D), lambda i: (i, 0)), pl.BlockSpec((1, D), lambda i: (0, 0)),
                  pl.BlockSpec((D, LANES), lambda i: (0, 0))],
        out_specs=[pl.BlockSpec((tr, LANES), lambda i: (i, 0)), pl.BlockSpec((tr, LANES), lambda i: (i, 0))],
        out_shape=[jax.ShapeDtypeStruct((M, LANES), I32), jax.ShapeDtypeStruct((M, LANES), F32)],
        compiler_params=_params(("parallel",)),
        name="route_top2",
    )(x, w, router_pad)
    return idx[:, :TOP_K], gate[:, :TOP_K]


def _first_of_weight(te_ref, m):
    prev = te_ref[jnp.maximum(m - 1, 0)]
    return jnp.logical_or(m == 0, te_ref[m] != prev)


def _for_rows(mode_ref, m, tail, fn, o_ref):
    pl.when(mode_ref[m] == MODE_FULL)(lambda: fn(slice(None)))
    if tail:
        pl.when(mode_ref[m] == MODE_TAIL)(lambda: fn(slice(0, tail)))
    pl.when(mode_ref[m] == MODE_SKIP)(lambda: o_ref.__setitem__(Ellipsis, jnp.zeros(o_ref.shape, o_ref.dtype)))


def _proj_body(te_ref, mode_ref, a_ref, w_ref, o_ref, wb_ref, *, tail):
    m = pl.program_id(1)
    pl.when(_first_of_weight(te_ref, m))(lambda: wb_ref.__setitem__(Ellipsis, w_ref[...].astype(BF16)))

    def rows(sl):
        o_ref[sl, :] = jnp.dot(a_ref[sl, :], wb_ref[...], preferred_element_type=F32)

    _for_rows(mode_ref, m, tail, rows, o_ref)


def matmul_proj(a, w, te, mode, *, n_lo, n_cols, tm, tn, tail=0):
    M, K = a.shape
    nb0 = n_lo // tn
    assert nb0 * tn == n_lo
    grid = (_cdiv(n_cols, tn), _cdiv(M, tm))
    return pl.pallas_call(
        functools.partial(_proj_body, tail=tail),
        grid_spec=pltpu.PrefetchScalarGridSpec(
            num_scalar_prefetch=2, grid=grid,
            in_specs=[pl.BlockSpec((tm, K), lambda n, m, te, md: (m, 0)),
                      pl.BlockSpec((None, K, tn), lambda n, m, te, md: (te[m], 0, nb0 + n))],
            out_specs=pl.BlockSpec((tm, tn), lambda n, m, te, md: (m, n)),
            scratch_shapes=[pltpu.VMEM((K, tn), BF16)]),
        out_shape=jax.ShapeDtypeStruct((M, n_cols), F32),
        compiler_params=_params(("arbitrary", "arbitrary")),
        name="matmul_proj",
    )(te, mode, a, w)


def _ffn1_body(te_ref, mode_ref, a_ref, w1_ref, w3_ref, o_ref, wb1_ref, wb3_ref, *, tail):
    m = pl.program_id(1)

    def cast():
        wb1_ref[...] = w1_ref[...].astype(BF16)
        wb3_ref[...] = w3_ref[...].astype(BF16)

    pl.when(_first_of_weight(te_ref, m))(cast)

    def rows(sl):
        a = a_ref[sl, :]
        u1 = jnp.dot(a, wb1_ref[...], preferred_element_type=F32)
        u3 = jnp.dot(a, wb3_ref[...], preferred_element_type=F32)
        o_ref[sl, :] = (u1 * jax.nn.sigmoid(u1) * u3).astype(o_ref.dtype)

    _for_rows(mode_ref, m, tail, rows, o_ref)


def matmul_swiglu_in(a, w1, w3, te, mode, *, tm, tn, tail=0):
    M, K = a.shape
    F = w1.shape[-1]
    grid = (_cdiv(F, tn), _cdiv(M, tm))
    wspec = pl.BlockSpec((None, K, tn), lambda n, m, te, md: (te[m], 0, n))
    return pl.pallas_call(
        functools.partial(_ffn1_body, tail=tail),
        grid_spec=pltpu.PrefetchScalarGridSpec(
            num_scalar_prefetch=2, grid=grid,
            in_specs=[pl.BlockSpec((tm, K), lambda n, m, te, md: (m, 0)), wspec, wspec],
            out_specs=pl.BlockSpec((tm, tn), lambda n, m, te, md: (m, n)),
            scratch_shapes=[pltpu.VMEM((K, tn), BF16), pltpu.VMEM((K, tn), BF16)]),
        out_shape=jax.ShapeDtypeStruct((M, F), BF16),
        compiler_params=_params(("arbitrary", "arbitrary")),
        name="matmul_swiglu_in",
    )(te, mode, a, w1, w3)


def _res_body(te_ref, mode_ref, a_ref, w_ref, res_ref, *rest, tail, scaled):
    if scaled:
        scale_ref, o_ref, wb_ref = rest
    else:
        o_ref, wb_ref = rest
    m = pl.program_id(1)
    pl.when(_first_of_weight(te_ref, m))(lambda: wb_ref.__setitem__(Ellipsis, w_ref[...].astype(BF16)))

    def rows(sl):
        d = jnp.dot(a_ref[sl, :], wb_ref[...], preferred_element_type=F32)
        if scaled:
            d = d * scale_ref[sl, :]
        o_ref[sl, :] = res_ref[sl, :] + d

    _for_rows(mode_ref, m, tail, rows, o_ref)


def matmul_residual(a, w, res, te, mode, *, tm, tn, tk, tail=0, scale=None):
    M, K = a.shape
    N = w.shape[-1]
    assert K % tk == 0 and N % tn == 0
    grid = (N // tn, _cdiv(M, tm))
    for kb in range(K // tk):
        in_specs = [pl.BlockSpec((tm, tk), lambda n, m, te, md, kb=kb: (m, kb)),
                    pl.BlockSpec((None, tk, tn), lambda n, m, te, md, kb=kb: (te[m], kb, n)),
                    pl.BlockSpec((tm, tn), lambda n, m, te, md: (m, n))]
        args = [te, mode, a, w, res]
        if scale is not None:
            in_specs.append(pl.BlockSpec((tm, 1), lambda n, m, te, md: (m, 0)))
            args.append(scale)
        res = pl.pallas_call(
            functools.partial(_res_body, tail=tail, scaled=scale is not None),
            grid_spec=pltpu.PrefetchScalarGridSpec(
                num_scalar_prefetch=2, grid=grid, in_specs=in_specs,
                out_specs=pl.BlockSpec((tm, tn), lambda n, m, te, md: (m, n)),
                scratch_shapes=[pltpu.VMEM((tk, tn), BF16)]),
            out_shape=jax.ShapeDtypeStruct((M, N), F32),
            input_output_aliases={4: 0},
            compiler_params=_params(("arbitrary", "arbitrary")),
            name="matmul_residual",
        )(*args)
    return res


def _dense_tiles(M, tm, e):
    nm = _cdiv(M, tm)
    tail = M - (nm - 1) * tm
    te = jnp.full((nm,), e, I32)
    if tail == tm:
        return te, jnp.full((nm,), MODE_FULL, I32), 0
    mode = jnp.array([MODE_FULL] * (nm - 1) + [MODE_TAIL], I32)
    return te, mode, tail


_hp_dot = functools.partial(jnp.dot, preferred_element_type=F32, precision=lax.Precision.HIGHEST)


def _small_body(*refs, kind, normed):
    it = iter(refs)
    x_ref = next(it)
    nw_ref = next(it) if normed else None
    a = x_ref[...]
    if normed:
        a = a * lax.rsqrt(jnp.mean(a * a, axis=-1, keepdims=True) + RMS_EPS) * nw_ref[...]
    if kind == 'swiglu':
        w1_ref, w3_ref, o_ref = it
        u1 = _hp_dot(a, w1_ref[...])
        u3 = _hp_dot(a, w3_ref[...])
        o_ref[...] = u1 * jax.nn.sigmoid(u1) * u3
    elif kind == 'res':
        w_ref, res_ref, o_ref = it
        o_ref[...] = res_ref[...] + _hp_dot(a, w_ref[...])
    else:
        w_ref, o_ref = it
        o_ref[...] = _hp_dot(a, w_ref[...])


def small_matmul(kind, x, ws, e, *, tn, norm_w=None, res=None, n_lo=0, n_cols=None):
    R, K = x.shape
    N = ws[0].shape[-1]
    n_cols = N - n_lo if n_cols is None else n_cols
    nb0 = n_lo // tn
    assert nb0 * tn == n_lo
    normed = norm_w is not None
    row = lambda w: pl.BlockSpec((R, w), lambda n: (0, 0))
    in_specs, args = [row(K)], [x]
    if normed:
        in_specs.append(pl.BlockSpec((1, K), lambda n: (0, 0)))
        args.append(norm_w)
    for w in ws:
        in_specs.append(pl.BlockSpec((None, K, tn), lambda n: (e, 0, nb0 + n)))
        args.append(w)
    ospec = pl.BlockSpec((R, tn), lambda n: (0, n))
    aliases = {}
    if kind == 'res':
        in_specs.append(ospec)
        aliases = {len(args): 0}
        args.append(res)
    return pl.pallas_call(
        functools.partial(_small_body, kind=kind, normed=normed),
        grid=(_cdiv(n_cols, tn),), in_specs=in_specs, out_specs=ospec,
        out_shape=jax.ShapeDtypeStruct((R, n_cols), F32),
        input_output_aliases=aliases,
        compiler_params=_params(("arbitrary",)),
        name="small_matmul_" + kind,
    )(*args)


def _gather_norm_body(src_ref, xp_hbm, xs_hbm, w_ref, o_ref, buf, sem, *, tg, mp):
    def issue(r, c):
        t = src_ref[r]
        dst = buf.at[pl.ds(r, 1), :]
        pl.when(t < mp)(lambda: pltpu.make_async_copy(xp_hbm.at[pl.ds(t, 1), :], dst, sem).start())
        pl.when(t >= mp)(lambda: pltpu.make_async_copy(xs_hbm.at[pl.ds(t - mp, 1), :], dst, sem).start())
        return c

    lax.fori_loop(0, tg, issue, 0, unroll=DMA_UNROLL)

    def wait(r, c):
        pltpu.make_async_copy(xp_hbm.at[pl.ds(0, 1), :], buf.at[pl.ds(r, 1), :], sem).wait()
        return c

    lax.fori_loop(0, tg, wait, 0, unroll=DMA_UNROLL)
    x = buf[...]
    ms = jnp.mean(x * x, axis=-1, keepdims=True)
    o_ref[...] = (x * lax.rsqrt(ms + RMS_EPS) * w_ref[...]).astype(o_ref.dtype)


def gather_rmsnorm_bf16(xp, xs, w, src, tg=256):
    Mp, D = xp.shape
    R = src.shape[0]
    assert R % tg == 0
    return pl.pallas_call(
        functools.partial(_gather_norm_body, tg=tg, mp=Mp),
        grid=(R // tg,),
        in_specs=[pl.BlockSpec((tg,), lambda i: (i,), memory_space=pltpu.SMEM),
                  pl.BlockSpec(memory_space=pl.ANY), pl.BlockSpec(memory_space=pl.ANY),
                  pl.BlockSpec((1, D), lambda i: (0, 0))],
        out_specs=pl.BlockSpec((tg, D), lambda i: (i, 0)),
        out_shape=jax.ShapeDtypeStruct((R, D), BF16),
        scratch_shapes=[pltpu.VMEM((tg, D), F32), pltpu.SemaphoreType.DMA(())],
        compiler_params=_params(("arbitrary",)),
        name="gather_rmsnorm_bf16",
    )(src, xp, xs, w)


def _combine_body(s0_ref, s1_ref, x_ref, y_hbm, o_ref, buf0, buf1, sem, *, tc):
    def issue(r, c):
        pltpu.make_async_copy(y_hbm.at[pl.ds(s0_ref[r], 1), :], buf0.at[pl.ds(r, 1), :], sem).start()
        pltpu.make_async_copy(y_hbm.at[pl.ds(s1_ref[r], 1), :], buf1.at[pl.ds(r, 1), :], sem).start()
        return c

    lax.fori_loop(0, tc, issue, 0, unroll=DMA_UNROLL)

    def wait(r, c):
        pltpu.make_async_copy(y_hbm.at[pl.ds(0, 1), :], buf0.at[pl.ds(r, 1), :], sem).wait()
        pltpu.make_async_copy(y_hbm.at[pl.ds(0, 1), :], buf1.at[pl.ds(r, 1), :], sem).wait()
        return c

    lax.fori_loop(0, tc, wait, 0, unroll=DMA_UNROLL)
    o_ref[...] = x_ref[...] + (buf0[...] + buf1[...])


def combine_top2(x, ys, slot0, slot1, tc=256):
    M, D = x.shape
    nt = _cdiv(M, tc)
    assert slot0.shape[0] == nt * tc
    sspec = pl.BlockSpec((tc,), lambda i: (i,), memory_space=pltpu.SMEM)
    return pl.pallas_call(
        functools.partial(_combine_body, tc=tc),
        grid=(nt,),
        in_specs=[sspec, sspec, pl.BlockSpec((tc, D), lambda i: (i, 0)), pl.BlockSpec(memory_space=pl.ANY)],
        out_specs=pl.BlockSpec((tc, D), lambda i: (i, 0)),
        out_shape=jax.ShapeDtypeStruct((M, D), F32),
        scratch_shapes=[pltpu.VMEM((tc, D), F32), pltpu.VMEM((tc, D), F32), pltpu.SemaphoreType.DMA(())],
        compiler_params=_params(("arbitrary",)),
        name="combine_top2",
    )(slot0, slot1, x, ys)


def _route_tables(idx, gates, tmg, n_tiles):
    M = idx.shape[0]
    e = idx.reshape(-1)
    onehot = (e[:, None] == jnp.arange(N_EXPERTS, dtype=I32)[None, :]).astype(I32)
    csum = jnp.cumsum(onehot, axis=0)
    rank = jnp.take_along_axis(csum, e[:, None], axis=1)[:, 0] - 1
    counts = csum[-1]
    tiles_e = (counts + tmg - 1) // tmg
    tile_end = jnp.cumsum(tiles_e)
    tile_start = tile_end - tiles_e
    slot = tile_start[e] * tmg + rank
    tiles = jnp.arange(n_tiles, dtype=I32)
    te_raw = jnp.sum((tiles[:, None] >= tile_end[None, :]).astype(I32), axis=1)
    total = tile_end[-1]
    valid = tiles < total
    te_last = jnp.sum((total - 1 >= tile_end).astype(I32))
    te = jnp.where(valid, te_raw, te_last).astype(I32)
    mode = jnp.where(valid, MODE_FULL, MODE_SKIP).astype(I32)
    rows = n_tiles * tmg
    src = jnp.zeros((rows,), I32).at[slot].set(jnp.arange(2 * M, dtype=I32) // 2)
    gate_s = jnp.zeros((rows,), F32).at[slot].set(gates.reshape(-1))
    return te, mode, src, gate_s[:, None], slot.reshape(M, TOP_K)


def _rope_tables(pos):
    half = ROPE_DIM // 2
    inv_freq = jnp.exp(-math.log(ROPE_THETA) * 2.0 * jnp.arange(half, dtype=F32) / ROPE_DIM)
    ang = pos.astype(F32)[:, None] * inv_freq[None, :]
    cos, sin = jnp.cos(ang), jnp.sin(ang)
    T = pos.shape[0]
    rest = HEAD_DIM - ROPE_DIM
    ctab = jnp.concatenate([cos, cos, jnp.ones((T, rest), F32)], axis=1)
    stab = jnp.concatenate([-sin, sin, jnp.zeros((T, rest), F32)], axis=1)
    d = jnp.arange(HEAD_DIM)
    partner = jnp.where(d < half, d + half, jnp.where(d < ROPE_DIM, d - half, d))
    swap = (d[:, None] == partner[None, :]).astype(BF16)
    return ctab, stab, swap


def _head_norm_rope(x, nw, ctab, stab, swap):
    y = x * lax.rsqrt(jnp.mean(x * x, axis=-1, keepdims=True) + RMS_EPS) * nw
    return y * ctab + _dot_exact_rhs(y, swap) * stab


def _attn_prompt_body(q_ref, kc_ref, kp_ref, vc_ref, vp_ref, cc_ref, sc_ref, cp_ref, sp_ref, qn_ref, kn_ref,
                      sink_ref, swap_ref, o_ref, knew_ref):
    i = pl.program_id(1)
    swap = swap_ref[...]
    cc, sc, cp, sp = cc_ref[...], sc_ref[...], cp_ref[...], sp_ref[...]
    qn, kn = qn_ref[...], kn_ref[...]
    R = Q_PER_KV * BLOCK
    row = lax.broadcasted_iota(I32, (R, 2 * BLOCK), 0)
    col = lax.broadcasted_iota(I32, (R, 2 * BLOCK), 1)
    rel = (row % BLOCK) - col + BLOCK
    mask = (rel >= 0) & (rel <= WINDOW) & jnp.logical_not((i == 0) & (col < BLOCK))
    rowh = lax.broadcasted_iota(I32, (R, 1), 0) // BLOCK
    for g in range(N_KV_HEADS):
        ks = slice(g * HEAD_DIM, (g + 1) * HEAD_DIM)
        k_cur = _head_norm_rope(kc_ref[:, ks], kn, cc, sc, swap)
        k_prev = _head_norm_rope(kp_ref[:, ks], kn, cp, sp, swap)
        knew_ref[:, ks] = k_cur
        kcat = jnp.concatenate([k_prev, k_cur], axis=0).astype(BF16)
        vcat = jnp.concatenate([vp_ref[:, ks], vc_ref[:, ks]], axis=0).astype(BF16)
        qs = []
        sink = jnp.zeros((R, 1), F32)
        for j in range(Q_PER_KV):
            h = g * Q_PER_KV + j
            qs.append(_head_norm_rope(q_ref[:, h * HEAD_DIM:(h + 1) * HEAD_DIM], qn, cc, sc, swap))
            sink = jnp.where(rowh == j, sink_ref[h], sink)
        qcat = jnp.concatenate(qs, axis=0).astype(BF16)
        s = lax.dot_general(qcat, kcat, (((1,), (1,)), ((), ())), preferred_element_type=F32) * ATT_SCALE
        s = jnp.where(mask, s, -jnp.inf)
        m = jnp.maximum(jnp.max(s, axis=-1, keepdims=True), sink)
        p = jnp.exp(s - m)
        denom = jnp.sum(p, axis=-1, keepdims=True) + jnp.exp(sink - m)
        o = jnp.dot((p / denom).astype(BF16), vcat, preferred_element_type=F32)
        for j in range(Q_PER_KV):
            h = g * Q_PER_KV + j
            o_ref[:, h * HEAD_DIM:(h + 1) * HEAD_DIM] = o[j * BLOCK:(j + 1) * BLOCK].astype(o_ref.dtype)


def attention_prompt(u_att, B, T, q_norm, k_norm, sinks):
    nb = T // BLOCK
    ctab, stab, swap = _rope_tables(jnp.arange(T, dtype=I32))
    kcol, vcol = ATT_WIDTH // KV_WIDTH, ATT_WIDTH // KV_WIDTH + 1
    cur = lambda b, i: b * nb + i
    prev = lambda b, i: b * nb + jnp.maximum(i - 1, 0)
    tspec_c = pl.BlockSpec((BLOCK, HEAD_DIM), lambda b, i: (i, 0))
    tspec_p = pl.BlockSpec((BLOCK, HEAD_DIM), lambda b, i: (jnp.maximum(i - 1, 0), 0))
    wspec = pl.BlockSpec((1, HEAD_DIM), lambda b, i: (0, 0))
    return pl.pallas_call(
        _attn_prompt_body,
        grid=(B, nb),
        in_specs=[pl.BlockSpec((BLOCK, ATT_WIDTH), lambda b, i: (cur(b, i), 0)),
                  pl.BlockSpec((BLOCK, KV_WIDTH), lambda b, i: (cur(b, i), kcol)),
                  pl.BlockSpec((BLOCK, KV_WIDTH), lambda b, i: (prev(b, i), kcol)),
                  pl.BlockSpec((BLOCK, KV_WIDTH), lambda b, i: (cur(b, i), vcol)),
                  pl.BlockSpec((BLOCK, KV_WIDTH), lambda b, i: (prev(b, i), vcol)),
                  tspec_c, tspec_c, tspec_p, tspec_p, wspec, wspec,
                  pl.BlockSpec(memory_space=pltpu.SMEM),
                  pl.BlockSpec((HEAD_DIM, HEAD_DIM), lambda b, i: (0, 0))],
        out_specs=[pl.BlockSpec((BLOCK, ATT_WIDTH), lambda b, i: (cur(b, i), 0)),
                   pl.BlockSpec((BLOCK, KV_WIDTH), lambda b, i: (b, 0))],
        out_shape=[jax.ShapeDtypeStruct((B * T, ATT_WIDTH), BF16),
                   jax.ShapeDtypeStruct((B * BLOCK, KV_WIDTH), F32)],
        compiler_params=_params(("arbitrary", "arbitrary")),
        name="attention_prompt",
    )(u_att, u_att, u_att, u_att, u_att, ctab, stab, ctab, stab, q_norm[None, :], k_norm[None, :], sinks, swap)


def _attn_sample_body(u_ref, k_ref, v_ref, ct_ref, st_ref, qn_ref, kn_ref, sink_ref, swap_ref, rexp_ref, bmask_ref,
                      o_ref, newk_ref, newv_ref, q_s, kx_s, vx_s, kn_s):
    b = pl.program_id(0)
    rexp = rexp_ref[...]

    @pl.when(b == 0)
    def _():
        swap = swap_ref[...]
        ct, st = ct_ref[...], st_ref[...]
        for h in range(N_Q_HEADS):
            sl = slice(h * HEAD_DIM, (h + 1) * HEAD_DIM)
            q_s[:, sl] = _head_norm_rope(u_ref[:, sl], qn_ref[...], ct, st, swap)
        for g in range(N_KV_HEADS):
            sl = slice(g * HEAD_DIM, (g + 1) * HEAD_DIM)
            kn_s[:, sl] = _head_norm_rope(u_ref[:, ATT_WIDTH + g * HEAD_DIM:ATT_WIDTH + (g + 1) * HEAD_DIM],
                                          kn_ref[...], ct, st, swap)
        kx_s[...] = _dot_exact_rhs(kn_s[...], rexp)
        vx_s[...] = _dot_exact_rhs(u_ref[:, ATT_WIDTH + KV_WIDTH:], rexp)

    bmask = bmask_ref[...]
    qfull = q_s[pl.ds(b, 1), :] * bmask
    kb, vb = k_ref[...], v_ref[...]
    kexp = _dot_exact_rhs(kb, rexp)
    vexp = _dot_exact_rhs(vb, rexp)
    s = lax.dot_general(qfull, kexp, (((1,), (1,)), ((), ())), preferred_element_type=F32,
                        precision=lax.Precision.HIGHEST) * ATT_SCALE
    s_self = jnp.sum(qfull * kx_s[pl.ds(b, 1), :], axis=-1, keepdims=True) * ATT_SCALE
    sink = sink_ref[...]
    m = jnp.maximum(jnp.maximum(jnp.max(s, axis=-1, keepdims=True), s_self), sink)
    p = jnp.exp(s - m)
    p_self = jnp.exp(s_self - m)
    denom = jnp.sum(p, axis=-1, keepdims=True) + p_self + jnp.exp(sink - m)
    o = _hp_dot(p / denom, vexp)
    o = o + (p_self / denom) * vx_s[pl.ds(b, 1), :]
    o_ref[pl.ds(b, 1), :] = jnp.sum(o * bmask, axis=0, keepdims=True)
    last = lax.broadcasted_iota(I32, kb.shape, 0) == WINDOW - 1
    newk_ref[...] = jnp.where(last, kn_s[pl.ds(b, 1), :], pltpu.roll(kb, WINDOW - 1, axis=0))
    newv_ref[...] = jnp.where(last, u_ref[pl.ds(b, 1), ATT_WIDTH + KV_WIDTH:], pltpu.roll(vb, WINDOW - 1, axis=0))


def attention_sample(u_att, row0, Bs, cache_k, cache_v, layer, q_norm, k_norm, sinks):
    assert row0 % Bs == 0
    ctab, stab, swap = _rope_tables(jnp.full((1,), PAST_LEN, I32))
    gd = jnp.arange(KV_WIDTH)
    hd = jnp.arange(ATT_WIDTH)
    rexp = ((gd[:, None] // HEAD_DIM == hd[None, :] // (HEAD_DIM * Q_PER_KV))
            & (gd[:, None] % HEAD_DIM == hd[None, :] % HEAD_DIM)).astype(BF16)
    bmask = (jnp.arange(N_Q_HEADS)[:, None] == hd[None, :] // HEAD_DIM).astype(F32)
    full = lambda shape: pl.BlockSpec(shape, lambda b: (0,) * len(shape))
    cspec = pl.BlockSpec((None, None, WINDOW, KV_WIDTH), lambda b: (layer, b, 0, 0))
    ospec = pl.BlockSpec((None, WINDOW, KV_WIDTH), lambda b: (b, 0, 0))
    return pl.pallas_call(
        _attn_sample_body,
        grid=(Bs,),
        in_specs=[pl.BlockSpec((Bs, ATT_PROJ), lambda b: (row0 // Bs, 0)), cspec, cspec,
                  full((1, HEAD_DIM)), full((1, HEAD_DIM)), full((1, HEAD_DIM)), full((1, HEAD_DIM)),
                  full((N_Q_HEADS, 1)), full((HEAD_DIM, HEAD_DIM)), full((KV_WIDTH, ATT_WIDTH)),
                  full((N_Q_HEADS, ATT_WIDTH))],
        out_specs=[full((Bs, ATT_WIDTH)), ospec, ospec],
        out_shape=[jax.ShapeDtypeStruct((Bs, ATT_WIDTH), F32),
                   jax.ShapeDtypeStruct((Bs, WINDOW, KV_WIDTH), F32),
                   jax.ShapeDtypeStruct((Bs, WINDOW, KV_WIDTH), F32)],
        scratch_shapes=[pltpu.VMEM((Bs, ATT_WIDTH), F32), pltpu.VMEM((Bs, ATT_WIDTH), F32),
                        pltpu.VMEM((Bs, ATT_WIDTH), F32), pltpu.VMEM((Bs, KV_WIDTH), F32)],
        compiler_params=_params(("arbitrary",)),
        name="attention_sample",
    )(u_att, cache_k, cache_v, ctab, stab, q_norm[None, :], k_norm[None, :], sinks[:, None], swap, rexp, bmask)


def _head_sum_matrix():
    i = jnp.arange(GROUP)
    return (i[:, None] // RWKV_HEAD == i[None, :] // RWKV_HEAD).astype(BF16)


def _head_sum(x, gmat):
    return jnp.concatenate([_dot_exact_rhs(x[:, q * GROUP:(q + 1) * GROUP], gmat) for q in range(N_GROUPS)], axis=1)


def _rwkv_prep_body(*refs, shifted, has_vres):
    it = iter(refs)
    u_ref = next(it)
    prev_ref = next(it)
    mu_ref, w0_ref, a0_ref, kk_ref, ka_ref, rk_ref, lora_ref, g_ref = (next(it) for _ in range(8))
    if has_vres:
        v0_ref, v1_ref, v2_ref, vf_ref = (next(it) for _ in range(4))
    r_o, w_o, k_o, v_o, a_o, b_o, g_o, bonus_o, tail_s = it
    u = u_ref[...]
    if shifted:
        first = jnp.where(pl.program_id(1) > 0, prev_ref[SUBLANES - 1:SUBLANES, :], 0.0)
        prev = jnp.where(lax.broadcasted_iota(I32, u.shape, 0) == 0, first, pltpu.roll(u, 1, axis=0))
    else:
        prev = prev_ref[...]
    xs = u + (prev - u) * mu_ref[...]
    r = xs[:, :RWKV_WIDTH]
    k = xs[:, RWKV_WIDTH:2 * RWKV_WIDTH]
    v = xs[:, 2 * RWKV_WIDTH:3 * RWKV_WIDTH]
    tail_s[...] = jnp.zeros(tail_s.shape, F32)
    tail_s[:, :LORA_IN] = xs[:, 3 * RWKV_WIDTH:]
    t = tail_s[...]
    def dot(x, w):
        if w.dtype == F32:
            return _hp_dot(x, w)
        return jnp.dot(x.astype(BF16), w, preferred_element_type=F32)

    dw = dot(jnp.tanh(t), lora_ref[0])
    da = dot(t, lora_ref[1])
    g = dot(jax.nn.sigmoid(t), lora_ref[2])
    z = -(w0_ref[...] + dw)
    softplus = jnp.maximum(z, 0.0) + jnp.log(1.0 + jnp.exp(-jnp.abs(z)))
    w_log = -softplus - 0.5
    decay = jnp.exp(-jnp.exp(w_log))
    a = jax.nn.sigmoid(a0_ref[...] + da)
    if has_vres:
        lo = dot(dot(v, v1_ref[...]), v2_ref[...])
        v = v + (vf_ref[...] - v) * jax.nn.sigmoid(v0_ref[...] + lo)
    gmat = g_ref[...]
    kk = k * kk_ref[...]
    kk = kk / jnp.maximum(jnp.sqrt(_head_sum(kk * kk, gmat)), 1e-12)
    k = k * (1.0 + (a - 1.0) * ka_ref[...])
    r_o[...] = r
    w_o[...] = decay
    k_o[...] = k
    v_o[...] = v
    a_o[...] = -kk
    b_o[...] = kk * a
    g_o[...] = g
    bonus_o[...] = _head_sum(r * k * rk_ref[...], gmat) * v


def rwkv_prep(u_rw, row0, B, T, prev_rows, p, v_first, tr):
    n = B * T
    shifted = prev_rows is None
    if shifted:
        assert T % tr == 0 and row0 == 0
        nt = T // tr
        grid = (B, nt)
        rowblk = lambda b, i: b * nt + i
        prev_spec = pl.BlockSpec((SUBLANES, SHIFT_WIDTH),
                                 lambda b, i: (jnp.maximum(rowblk(b, i) * (tr // SUBLANES) - 1, 0), 0))
        prev_arr = u_rw
        oblk = lambda b, i: (b * nt + i, 0)
    else:
        assert T == 1 and tr == B and row0 % tr == 0
        grid = (1, 1)
        rowblk = lambda b, i: row0 // tr
        prev_spec = pl.BlockSpec((tr, SHIFT_WIDTH), lambda b, i: (0, 0))
        prev_arr = prev_rows
        oblk = lambda b, i: (0, 0)
    vec = lambda w: pl.BlockSpec((1, w), lambda b, i: (0, 0))
    in_specs = [pl.BlockSpec((tr, SHIFT_WIDTH), lambda b, i: (rowblk(b, i), 0)), prev_spec,
                vec(SHIFT_WIDTH), vec(RWKV_WIDTH), vec(RWKV_WIDTH), vec(RWKV_WIDTH), vec(RWKV_WIDTH),
                vec(RWKV_WIDTH), pl.BlockSpec((3, LORA_PAD, RWKV_WIDTH), lambda b, i: (0, 0, 0)),
                pl.BlockSpec((GROUP, GROUP), lambda b, i: (0, 0))]
    args = [u_rw, prev_arr, p['mu'], p['w0'], p['a0'], p['k_k'], p['k_a'], p['r_k'], p['lora'], _head_sum_matrix()]
    has_vres = v_first is not None
    if has_vres:
        in_specs += [vec(RWKV_WIDTH), pl.BlockSpec((RWKV_WIDTH, MV_LORA), lambda b, i: (0, 0)),
                     pl.BlockSpec((MV_LORA, RWKV_WIDTH), lambda b, i: (0, 0)),
                     pl.BlockSpec((tr, RWKV_WIDTH), oblk)]
        args += [p['v0'], p['v1'], p['v2'], v_first]
    ospec = pl.BlockSpec((tr, RWKV_WIDTH), oblk)
    return pl.pallas_call(
        functools.partial(_rwkv_prep_body, shifted=shifted, has_vres=has_vres),
        grid=grid, in_specs=in_specs, out_specs=[ospec] * 8,
        out_shape=[jax.ShapeDtypeStruct((n, RWKV_WIDTH), F32)] * 8,
        scratch_shapes=[pltpu.VMEM((tr, LORA_PAD), F32)],
        compiler_params=_params(("arbitrary", "arbitrary")),
        name="rwkv_prep",
    )(*args)


def _rwkv_params(l, shift_mu, w0, w2, a0, a2, g2, k_k, k_a, r_k, v0, v1, v2):
    lora = jnp.zeros((3, LORA_PAD, RWKV_WIDTH), F32)
    lora = lora.at[0, :DECAY_LORA].set(w2[l])
    lora = lora.at[1, DECAY_LORA:DECAY_LORA + AAA_LORA].set(a2[l])
    lora = lora.at[2, DECAY_LORA + AAA_LORA:LORA_IN].set(g2[l])
    p = {'mu': shift_mu[l][None, :], 'w0': w0[l][None, :], 'a0': a0[l][None, :], 'k_k': k_k[l][None, :],
         'k_a': k_a[l][None, :], 'r_k': r_k[l].reshape(1, RWKV_WIDTH), 'lora': lora}
    if l > 0:
        p['v0'] = v0[l - 1][None, :]
        p['v1'] = v1[l - 1]
        p['v2'] = v2[l - 1]
    p_bf16 = {k_: (v_.astype(BF16) if k_ in ('lora', 'v1', 'v2') else v_) for k_, v_ in p.items()}
    return p, p_bf16


def _wkv_body(r_ref, w_ref, k_ref, v_ref, a_ref, b_ref, s0_ref, g_ref, eye_ref, y_ref, sout_ref, s_s, sb_s, *, nb,
              tc, precise):
    c = pl.program_id(1)

    @pl.when(c == 0)
    def _():
        s_s[...] = s0_ref[...]
        sb_s[...] = s0_ref[...].astype(BF16)

    gmat = g_ref[...]
    eye = eye_ref[...][None]
    eye_b = eye.astype(BF16)
    rows = nb * RWKV_HEAD
    dot = functools.partial(jnp.dot, preferred_element_type=F32)

    groups = [slice(q * GROUP, (q + 1) * GROUP) for q in range(N_GROUPS)]

    def block(red, n):
        return red[n * rows:(n + 1) * rows].reshape(nb, RWKV_HEAD, GROUP)

    def diagonal(yx):
        d = (yx * eye).reshape(nb, RWKV_HEAD // SUBLANES, SUBLANES, GROUP)
        return jnp.sum(jnp.sum(d, axis=1), axis=1, keepdims=True)

    def store_y(t, yx):
        for q, sl in enumerate(groups):
            y_ref[:, pl.ds(t, 1), sl] = diagonal(block(yx, q))

    def step_bf16(t, carry):
        tp = jnp.maximum(t - 1, 0)
        w_t, k_t, v_t, a_t, b_t = (ref[:, pl.ds(t, 1), :] for ref in (w_ref, k_ref, v_ref, a_ref, b_ref))
        a_b, v_b, r_b = a_t.astype(BF16), v_t.astype(BF16), r_ref[:, pl.ds(tp, 1), :].astype(BF16)
        tiles = (nb, RWKV_HEAD // BF16_ROWS, BF16_ROWS, GROUP)
        lhs = []
        for q, sl in enumerate(groups):
            sb = sb_s[q]
            parts = [(sb * a_b[:, :, sl]).reshape(tiles), (eye_b * v_b[:, :, sl]).reshape(tiles),
                     (sb * r_b[:, :, sl]).reshape(tiles)]
            lhs.append(jnp.stack(parts, axis=2).reshape(3 * rows, GROUP))
        red = dot(jnp.concatenate(lhs, axis=0), gmat)
        ys = []
        for q, sl in enumerate(groups):
            rq = red[3 * q * rows:3 * (q + 1) * rows].reshape(tiles[:2] + (3,) + tiles[2:])
            sa, vx = (rq[:, :, j].reshape(nb, RWKV_HEAD, GROUP) for j in range(2))
            s_new = s_s[q] * w_t[:, :, sl] + sa * b_t[:, :, sl] + vx * k_t[:, :, sl]
            s_s[q] = s_new
            sb_s[q] = s_new.astype(BF16)
            ys.append(diagonal(rq[:, :, 2].reshape(nb, RWKV_HEAD, GROUP)))

        @pl.when(t > 0)
        def _():
            for q, sl in enumerate(groups):
                y_ref[:, pl.ds(tp, 1), sl] = ys[q]

        return carry

    def last_y_bf16():
        r_b = r_ref[:, pl.ds(tc - 1, 1), :].astype(BF16)
        lhs = [(sb_s[q] * r_b[:, :, sl]).reshape(rows, GROUP) for q, sl in enumerate(groups)]
        store_y(tc - 1, dot(jnp.concatenate(lhs, axis=0), gmat))

    def head_sums3(xs):
        lhs = []
        for x in xs:
            lhs += list(_split3(x.reshape(rows, GROUP)))
        red = dot(jnp.concatenate(lhs, axis=0), gmat)
        out = []
        for q in range(len(xs)):
            o = 3 * q * rows
            acc = red[o:o + rows] + red[o + rows:o + 2 * rows] + red[o + 2 * rows:o + 3 * rows]
            out.append(acc.reshape(nb, RWKV_HEAD, GROUP))
        return out

    def step_f32(t, carry):
        r_t, w_t, k_t, v_t, a_t, b_t = (ref[:, pl.ds(t, 1), :] for ref in (r_ref, w_ref, k_ref, v_ref, a_ref, b_ref))
        groups = [slice(q * GROUP, (q + 1) * GROUP) for q in range(N_GROUPS)]
        sa = head_sums3([s_s[q] * a_t[:, :, sl] for q, sl in enumerate(groups)])
        vx = head_sums3([eye * v_t[:, :, sl] for sl in groups])
        prods = []
        for q, sl in enumerate(groups):
            s_new = s_s[q] * w_t[:, :, sl] + sa[q] * b_t[:, :, sl] + vx[q] * k_t[:, :, sl]
            s_s[q] = s_new
            prods.append(s_new * r_t[:, :, sl])
        yx = head_sums3(prods)
        for q, sl in enumerate(groups):
            y_ref[:, pl.ds(t, 1), sl] = jnp.sum(yx[q] * eye, axis=1, keepdims=True)
        return carry

    if precise:
        lax.fori_loop(0, tc, step_f32, 0)
    else:
        lax.fori_loop(0, tc, step_bf16, 0)
        last_y_bf16()

    @pl.when(c == pl.num_programs(1) - 1)
    def _():
        sout_ref[...] = s_s[...]


def wkv_scan(r, w, k, v, a, b, s0, nb=4, tc=64, precise=False):
    B, T, W = r.shape
    tc = min(tc, T)
    assert B % nb == 0 and T % tc == 0
    xspec = pl.BlockSpec((nb, tc, W), lambda i, c: (i, c, 0))
    sspec = pl.BlockSpec((N_GROUPS, nb, RWKV_HEAD, GROUP), lambda i, c: (0, i, 0, 0))
    j = jnp.arange(GROUP)
    eye = (jnp.arange(RWKV_HEAD)[:, None] == j[None, :] % RWKV_HEAD).astype(F32)
    return pl.pallas_call(
        functools.partial(_wkv_body, nb=nb, tc=tc, precise=precise),
        grid=(B // nb, T // tc),
        in_specs=[xspec] * 6 + [sspec, pl.BlockSpec((GROUP, GROUP), lambda i, c: (0, 0)),
                                pl.BlockSpec((RWKV_HEAD, GROUP), lambda i, c: (0, 0))],
        out_specs=[xspec, sspec],
        out_shape=[jax.ShapeDtypeStruct((B, T, W), F32), jax.ShapeDtypeStruct(s0.shape, F32)],
        scratch_shapes=[pltpu.VMEM((N_GROUPS, nb, RWKV_HEAD, GROUP), F32),
                        pltpu.VMEM((N_GROUPS, nb, RWKV_HEAD, GROUP), BF16)],
        compiler_params=_params(("arbitrary", "arbitrary")),
        name="wkv_scan",
    )(r, w, k, v, a, b, s0, _head_sum_matrix(), eye)


def _state_to_groups(s):
    B = s.shape[0]
    return s.reshape(B, N_GROUPS, HEADS_PER_GROUP, RWKV_HEAD, RWKV_HEAD).transpose(1, 0, 3, 2, 4).reshape(
        N_GROUPS, B, RWKV_HEAD, GROUP)


def _state_from_groups(s):
    B = s.shape[1]
    return s.reshape(N_GROUPS, B, RWKV_HEAD, HEADS_PER_GROUP, RWKV_HEAD).transpose(1, 0, 3, 2, 4).reshape(
        B, N_RWKV_HEADS, RWKV_HEAD, RWKV_HEAD)


def _rwkv_post_body(y_ref, bonus_ref, g_ref, lw_ref, lb_ref, gm_ref, o_ref):
    gmat = gm_ref[...]
    y = y_ref[...]
    mean = _head_sum(y, gmat) * (1.0 / RWKV_HEAD)
    d = y - mean
    var = _head_sum(d * d, gmat) * (1.0 / RWKV_HEAD)
    yn = d * lax.rsqrt(var + GN_EPS) * lw_ref[...] + lb_ref[...]
    o_ref[...] = ((yn + bonus_ref[...]) * g_ref[...]).astype(o_ref.dtype)


def rwkv_post(y, bonus, g, ln_w, ln_b, out_dtype, tr=256):
    n, W = y.shape
    tr = min(tr, n)
    assert n % tr == 0
    spec = pl.BlockSpec((tr, W), lambda i: (i, 0))
    vec = pl.BlockSpec((1, W), lambda i: (0, 0))
    return pl.pallas_call(
        _rwkv_post_body, grid=(n // tr,),
        in_specs=[spec, spec, spec, vec, vec, pl.BlockSpec((GROUP, GROUP), lambda i: (0, 0))],
        out_specs=spec, out_shape=jax.ShapeDtypeStruct((n, W), out_dtype),
        compiler_params=_params(("parallel",)),
        name="rwkv_post",
    )(y, bonus, g, ln_w[None, :], ln_b[None, :], _head_sum_matrix())


def _rwkv_mixer(u_rw, B, T, prev_rows, s0_groups, p, v_first, ln_w, ln_b, tr, precise):
    r, w, k, v, a, b, g, bonus = rwkv_prep(u_rw, 0, B, T, prev_rows, p, v_first, tr)
    sh = lambda t: t.reshape(B, T, RWKV_WIDTH)
    y, s_out = wkv_scan(sh(r), sh(w), sh(k), sh(v), sh(a), sh(b), s0_groups, precise=precise)
    out = rwkv_post(y.reshape(B * T, RWKV_WIDTH), bonus, g, ln_w, ln_b, F32 if precise else BF16)
    return out, _state_from_groups(s_out), v


TM = 1024
TM_MOE = 512
TC = 256
TR_PREP = 128
TN_SMALL = 512


def kernel(x_prompt, x_sample, state_shift, state_wkv, cache_win_k, cache_win_v, attn_norm_w, w_in, q_norm_w,
           k_norm_w, attn_sinks, shift_mu, w0, w2, a0, a2, g2, k_k, k_a, r_k, ln_x_w, ln_x_b, v0, v1, v2, w_out,
           ffn_norm_w, dense_w1, dense_w3, dense_w2, router_w, moe_w1, moe_w3, moe_w2):
    Bp, Tp, D = x_prompt.shape
    Bs, Ts, _ = x_sample.shape
    assert Ts == 1 and cache_win_k.shape[2] == WINDOW and Tp % BLOCK == 0
    Mp, Ms = Bp * Tp, Bs * Ts
    assert Mp % TM == 0 and Mp % TC == 0
    xp = x_prompt.reshape(Mp, D)
    xs = x_sample.reshape(Ms, D)
    cache_k = cache_win_k.reshape(DEPTH, Bs, WINDOW, KV_WIDTH)
    cache_v = cache_win_v.reshape(DEPTH, Bs, WINDOW, KV_WIDTH)
    n_moe_tiles = (TOP_K * (Mp + Ms)) // TM_MOE + N_EXPERTS
    zero_state = jnp.zeros((N_GROUPS, Bp, RWKV_HEAD, GROUP), F32)
    vf_p = vf_s = None
    outs = {k_: [] for k_ in ('p_row', 'p_wkv', 'p_k', 'p_v', 's_row', 's_wkv', 's_k', 's_v')}
    for l in range(DEPTH):
        te, mode, _ = _dense_tiles(Mp, TM, l)
        p_f32, p_bf16 = _rwkv_params(l, shift_mu, w0, w2, a0, a2, g2, k_k, k_a, r_k, v0, v1, v2)
        an = attn_norm_w[l][None, :]
        h = rmsnorm_bf16(xp, an)
        u_att = matmul_proj(h, w_in, te, mode, n_lo=0, n_cols=ATT_PROJ, tm=TM, tn=512)
        u_rw = matmul_proj(h, w_in, te, mode, n_lo=ATT_PROJ, n_cols=SHIFT_WIDTH, tm=TM, tn=512)
        att_p, knew_p = attention_prompt(u_att, Bp, Tp, q_norm_w[l], k_norm_w[l], attn_sinks[l])
        rw_p, S_p, v_p_first = _rwkv_mixer(u_rw, Bp, Tp, None, zero_state, p_bf16, vf_p, ln_x_w[l], ln_x_b[l],
                                           TR_PREP, False)
        xp = matmul_residual(jnp.concatenate([att_p, rw_p], axis=1), w_out, xp, te, mode, tm=TM, tn=512, tk=D)
        us_att = small_matmul('proj', xs, [w_in], l, tn=TN_SMALL, norm_w=an, n_lo=0, n_cols=ATT_PROJ)
        us_rw = small_matmul('proj', xs, [w_in], l, tn=TN_SMALL, norm_w=an, n_lo=ATT_PROJ, n_cols=SHIFT_WIDTH)
        att_s, k_s, v_s = attention_sample(us_att, 0, Bs, cache_k, cache_v, l, q_norm_w[l], k_norm_w[l],
                                           attn_sinks[l])
        rw_s, S_s, v_s_first = _rwkv_mixer(us_rw, Bs, Ts, state_shift[l], _state_to_groups(state_wkv[l]), p_f32,
                                           vf_s, ln_x_w[l], ln_x_b[l], Bs, True)
        xs = small_matmul('res', jnp.concatenate([att_s, rw_s], axis=1), [w_out], l, tn=TN_SMALL, res=xs)
        if l == 0:
            vf_p, vf_s = v_p_first, v_s_first
        fn = ffn_norm_w[l][None, :]
        if l % 2 == 0:
            te, mode, _ = _dense_tiles(Mp, TM, l // 2)
            h = rmsnorm_bf16(xp, fn)
            g = matmul_swiglu_in(h, dense_w1, dense_w3, te, mode, tm=TM, tn=256)
            xp = matmul_residual(g, dense_w2, xp, te, mode, tm=TM, tn=512, tk=D_FF // 4)
            gs = small_matmul('swiglu', xs, [dense_w1, dense_w3], l // 2, tn=TN_SMALL // 2, norm_w=fn)
            xs = small_matmul('res', gs, [dense_w2], l // 2, tn=TN_SMALL // 4, res=xs)
        else:
            router_pad = jnp.pad(router_w[l // 2], ((0, 0), (0, LANES - N_EXPERTS)))
            idx_p, gates_p = route_top2(xp, fn, router_pad)
            idx_s, gates_s = route_top2(xs, fn, router_pad, tr=Ms)
            te, mode, src, gate_rows, slot = _route_tables(
                jnp.concatenate([idx_p, idx_s], axis=0), jnp.concatenate([gates_p, gates_s], axis=0),
                TM_MOE, n_moe_tiles)
            rows = gather_rmsnorm_bf16(xp, xs, fn, src, tg=TC)
            g = matmul_swiglu_in(rows, moe_w1[l // 2], moe_w3[l // 2], te, mode, tm=TM_MOE, tn=512)
            ys = matmul_residual(g, moe_w2[l // 2], jnp.zeros((rows.shape[0], D), F32), te, mode,
                                 tm=TM_MOE, tn=1024, tk=D_FF // 4, scale=gate_rows)
            xp = combine_top2(xp, ys, slot[:Mp, 0], slot[:Mp, 1], tc=TC)
            xs = combine_top2(xs, ys, slot[Mp:, 0], slot[Mp:, 1], tc=Ms)
        u_att_p = u_att.reshape(Bp, Tp, ATT_PROJ)
        outs['p_row'].append(u_rw.reshape(Bp, Tp, SHIFT_WIDTH)[:, Tp - 1])
        outs['p_wkv'].append(S_p)
        outs['p_k'].append(knew_p.reshape(Bp, WINDOW, N_KV_HEADS, HEAD_DIM))
        outs['p_v'].append(u_att_p[:, Tp - WINDOW:, ATT_WIDTH + KV_WIDTH:].reshape(Bp, WINDOW, N_KV_HEADS, HEAD_DIM))
        outs['s_row'].append(us_rw)
        outs['s_wkv'].append(S_s)
        outs['s_k'].append(k_s.reshape(Bs, WINDOW, N_KV_HEADS, HEAD_DIM))
        outs['s_v'].append(v_s.reshape(Bs, WINDOW, N_KV_HEADS, HEAD_DIM))
    return (xp.reshape(Bp, Tp, D), xs.reshape(Bs, Ts, D),
            jnp.stack(outs['p_row']), jnp.stack(outs['p_wkv']), jnp.stack(outs['p_k']), jnp.stack(outs['p_v']),
            jnp.stack(outs['s_row']), jnp.stack(outs['s_wkv']), jnp.stack(outs['s_k']), jnp.stack(outs['s_v']))
```

```python
import functools
import math

import jax
import jax.numpy as jnp
from jax import lax
from jax.experimental import pallas as pl
from jax.experimental.pallas import tpu as pltpu

F32 = jnp.float32
BF16 = jnp.bfloat16
I32 = jnp.int32

D_MODEL = 4096
DEPTH = 2
PAST_LEN = 16384
HEAD_DIM = 64
ATT_WIDTH = D_MODEL // 2
N_Q_HEADS = ATT_WIDTH // HEAD_DIM
N_KV_HEADS = 8
Q_PER_KV = N_Q_HEADS // N_KV_HEADS
KV_WIDTH = N_KV_HEADS * HEAD_DIM
ATT_PROJ = ATT_WIDTH + 2 * KV_WIDTH
WINDOW = 128
BLOCK = 128
ROPE_DIM = HEAD_DIM // 4
ROPE_THETA = 500000.0
ATT_SCALE = HEAD_DIM ** -0.5
RWKV_WIDTH = D_MODEL - ATT_WIDTH
RWKV_HEAD = 64
N_RWKV_HEADS = RWKV_WIDTH // RWKV_HEAD
DECAY_LORA = 96
AAA_LORA = 96
GATE_LORA = 256
MV_LORA = 64
LORA_IN = DECAY_LORA + AAA_LORA + GATE_LORA
SHIFT_WIDTH = 3 * RWKV_WIDTH + LORA_IN
D_FF = 14336
N_EXPERTS = 8
TOP_K = 2
RMS_EPS = 1e-5
GN_EPS = 64e-5

VMEM_LIMIT_BYTES = 58 * 2 ** 20
LANES = 128
SUBLANES = 8
BF16_ROWS = 2 * SUBLANES
LORA_PAD = 4 * LANES
GROUP = 2 * LANES
N_GROUPS = RWKV_WIDTH // GROUP
HEADS_PER_GROUP = GROUP // RWKV_HEAD

MODE_SKIP, MODE_FULL = 0, 1
MODE_PART = 8
PART_ROWS = LANES
DMA_UNROLL = 8


def _cdiv(a, b):
    return -(-a // b)


def _params(sem):
    return pltpu.CompilerParams(dimension_semantics=sem, vmem_limit_bytes=VMEM_LIMIT_BYTES)


def _split3(x):
    hi = x.astype(BF16)
    r1 = x - hi.astype(F32)
    mid = r1.astype(BF16)
    lo = (r1 - mid.astype(F32)).astype(BF16)
    return hi, mid, lo


def _dot_exact_rhs(x, rhs_bf16):
    hi, mid, lo = _split3(x)
    d = functools.partial(jnp.dot, preferred_element_type=F32)
    return d(hi, rhs_bf16) + d(mid, rhs_bf16) + d(lo, rhs_bf16)


def _rmsnorm_body(x_ref, w_ref, o_ref):
    x = x_ref[...]
    ms = jnp.mean(x * x, axis=-1, keepdims=True)
    o_ref[...] = (x * lax.rsqrt(ms + RMS_EPS) * w_ref[...]).astype(o_ref.dtype)


def rmsnorm_bf16(x, w, tr=512):
    M, D = x.shape
    return pl.pallas_call(
        _rmsnorm_body,
        grid=(_cdiv(M, tr),),
        in_specs=[pl.BlockSpec((tr, D), lambda i: (i, 0)), pl.BlockSpec((1, D), lambda i: (0, 0))],
        out_specs=pl.BlockSpec((tr, D), lambda i: (i, 0)),
        out_shape=jax.ShapeDtypeStruct((M, D), BF16),
        compiler_params=_params(("parallel",)),
        name="rmsnorm_bf16",
    )(x, w)


def _router_body(x_ref, w_ref, r_ref, idx_ref, gate_ref):
    x = x_ref[...]
    ms = jnp.mean(x * x, axis=-1, keepdims=True)
    h = x * lax.rsqrt(ms + RMS_EPS) * w_ref[...]
    logits = jnp.dot(h, r_ref[...], preferred_element_type=F32, precision=lax.Precision.HIGHEST)
    lane = lax.broadcasted_iota(I32, logits.shape, 1)
    neg = jnp.float32(-jnp.inf)
    l1 = jnp.where(lane < N_EXPERTS, logits, neg)
    m1 = jnp.max(l1, axis=-1, keepdims=True)
    i1 = jnp.min(jnp.where(l1 == m1, lane, LANES), axis=-1, keepdims=True)
    l2 = jnp.where(lane == i1, neg, l1)
    m2 = jnp.max(l2, axis=-1, keepdims=True)
    i2 = jnp.min(jnp.where(l2 == m2, lane, LANES), axis=-1, keepdims=True)
    e2 = jnp.exp(m2 - m1)
    den = 1.0 + e2
    idx_ref[...] = jnp.where(lane == 0, i1, jnp.where(lane == 1, i2, 0))
    gate_ref[...] = jnp.where(lane == 0, 1.0 / den, jnp.where(lane == 1, e2 / den, 0.0))


def route_top2(x, w, router_pad, tr=512):
    M, D = x.shape
    idx, gate = pl.pallas_call(
        _router_body,
        grid=(_cdiv(M, tr),),
        in_specs=[pl.BlockSpec((tr, D), lambda i: (i, 0)), pl.BlockSpec((1, D), lambda i: (0, 0)),
                  pl.BlockSpec((D, LANES), lambda i: (0, 0))],
        out_specs=[pl.BlockSpec((tr, LANES), lambda i: (i, 0)), pl.BlockSpec((tr, LANES), lambda i: (i, 0))],
        out_shape=[jax.ShapeDtypeStruct((M, LANES), I32), jax.ShapeDtypeStruct((M, LANES), F32)],
        compiler_params=_params(("parallel",)),
        name="route_top2",
    )(x, w, router_pad)
    return idx[:, :TOP_K], gate[:, :TOP_K]


def _first_of_weight(te_ref, m):
    prev = te_ref[jnp.maximum(m - 1, 0)]
    return jnp.logical_or(m == 0, te_ref[m] != prev)


def _for_rows(mode_ref, m, parts, fn, o_ref):
    mode = mode_ref[m]
    pl.when(mode == MODE_FULL)(lambda: fn(slice(None)))

    def zero(sl):
        o_ref[sl, :] = jnp.zeros(o_ref[sl, :].shape, o_ref.dtype)

    pl.when(mode == MODE_SKIP)(lambda: zero(slice(None)))
    if parts:
        tm = o_ref.shape[0]
        assert tm % PART_ROWS == 0
        for c in range(tm // PART_ROWS):
            sl = slice(c * PART_ROWS, (c + 1) * PART_ROWS)
            pl.when(jnp.logical_and(mode >= MODE_PART, c < mode - MODE_PART))(lambda sl=sl: fn(sl))
            pl.when(jnp.logical_and(mode >= MODE_PART, c >= mode - MODE_PART))(lambda sl=sl: zero(sl))


def _proj_body(te_ref, mode_ref, a_ref, w_ref, o_ref, wb_ref, *, parts):
    m = pl.program_id(1)
    pl.when(_first_of_weight(te_ref, m))(lambda: wb_ref.__setitem__(Ellipsis, w_ref[...].astype(BF16)))

    def rows(sl):
        o_ref[sl, :] = jnp.dot(a_ref[sl, :], wb_ref[...], preferred_element_type=F32)

    _for_rows(mode_ref, m, parts, rows, o_ref)


def matmul_proj(a, w, te, mode, *, n_lo, n_cols, tm, tn, parts=False):
    M, K = a.shape
    nb0 = n_lo // tn
    assert nb0 * tn == n_lo
    grid = (_cdiv(n_cols, tn), _cdiv(M, tm))
    return pl.pallas_call(
        functools.partial(_proj_body, parts=parts),
        grid_spec=pltpu.PrefetchScalarGridSpec(
            num_scalar_prefetch=2, grid=grid,
            in_specs=[pl.BlockSpec((tm, K), lambda n, m, te, md: (m, 0)),
                      pl.BlockSpec((None, K, tn), lambda n, m, te, md: (te[m], 0, nb0 + n))],
            out_specs=pl.BlockSpec((tm, tn), lambda n, m, te, md: (m, n)),
            scratch_shapes=[pltpu.VMEM((K, tn), BF16)]),
        out_shape=jax.ShapeDtypeStruct((M, n_cols), F32),
        compiler_params=_params(("arbitrary", "arbitrary")),
        name="matmul_proj",
    )(te, mode, a, w)


def _ffn1_body(te_ref, mode_ref, a_ref, w1_ref, w3_ref, o_ref, wb1_ref, wb3_ref, *, parts):
    m = pl.program_id(1)

    def cast():
        wb1_ref[...] = w1_ref[...].astype(BF16)
        wb3_ref[...] = w3_ref[...].astype(BF16)

    pl.when(_first_of_weight(te_ref, m))(cast)

    def rows(sl):
        a = a_ref[sl, :]
        u1 = jnp.dot(a, wb1_ref[...], preferred_element_type=F32)
        u3 = jnp.dot(a, wb3_ref[...], preferred_element_type=F32)
        o_ref[sl, :] = (u1 * jax.nn.sigmoid(u1) * u3).astype(o_ref.dtype)

    _for_rows(mode_ref, m, parts, rows, o_ref)


def matmul_swiglu_in(a, w1, w3, te, mode, *, tm, tn, parts=False):
    M, K = a.shape
    F = w1.shape[-1]
    grid = (_cdiv(F, tn), _cdiv(M, tm))
    wspec = pl.BlockSpec((None, K, tn), lambda n, m, te, md: (te[m], 0, n))
    return pl.pallas_call(
        functools.partial(_ffn1_body, parts=parts),
        grid_spec=pltpu.PrefetchScalarGridSpec(
            num_scalar_prefetch=2, grid=grid,
            in_specs=[pl.BlockSpec((tm, K), lambda n, m, te, md: (m, 0)), wspec, wspec],
            out_specs=pl.BlockSpec((tm, tn), lambda n, m, te, md: (m, n)),
            scratch_shapes=[pltpu.VMEM((K, tn), BF16), pltpu.VMEM((K, tn), BF16)]),
        out_shape=jax.ShapeDtypeStruct((M, F), BF16),
        compiler_params=_params(("arbitrary", "arbitrary")),
        name="matmul_swiglu_in",
    )(te, mode, a, w1, w3)


def _res_body(te_ref, mode_ref, a_ref, w_ref, res_ref, *rest, parts, scaled):
    if scaled:
        scale_ref, o_ref, wb_ref = rest
    else:
        o_ref, wb_ref = rest
    m = pl.program_id(1)
    pl.when(_first_of_weight(te_ref, m))(lambda: wb_ref.__setitem__(Ellipsis, w_ref[...].astype(BF16)))

    def rows(sl):
        d = jnp.dot(a_ref[sl, :], wb_ref[...], preferred_element_type=F32)
        if scaled:
            d = d * scale_ref[sl, :]
        o_ref[sl, :] = res_ref[sl, :] + d

    _for_rows(mode_ref, m, parts, rows, o_ref)


def matmul_residual(a, w, res, te, mode, *, tm, tn, tk, parts=False, scale=None):
    M, K = a.shape
    N = w.shape[-1]
    assert K % tk == 0 and N % tn == 0
    grid = (N // tn, _cdiv(M, tm))
    for kb in range(K // tk):
        in_specs = [pl.BlockSpec((tm, tk), lambda n, m, te, md, kb=kb: (m, kb)),
                    pl.BlockSpec((None, tk, tn), lambda n, m, te, md, kb=kb: (te[m], kb, n)),
                    pl.BlockSpec((tm, tn), lambda n, m, te, md: (m, n))]
        args = [te, mode, a, w, res]
        if scale is not None:
            in_specs.append(pl.BlockSpec((tm, 1), lambda n, m, te, md: (m, 0)))
            args.append(scale)
        res = pl.pallas_call(
            functools.partial(_res_body, parts=parts, scaled=scale is not None),
            grid_spec=pltpu.PrefetchScalarGridSpec(
                num_scalar_prefetch=2, grid=grid, in_specs=in_specs,
                out_specs=pl.BlockSpec((tm, tn), lambda n, m, te, md: (m, n)),
                scratch_shapes=[pltpu.VMEM((tk, tn), BF16)]),
            out_shape=jax.ShapeDtypeStruct((M, N), F32),
            input_output_aliases={4: 0},
            compiler_params=_params(("arbitrary", "arbitrary")),
            name="matmul_residual",
        )(*args)
    return res


def _dense_tiles(M, tm, e):
    assert M % tm == 0
    return jnp.full((M // tm,), e, I32), jnp.full((M // tm,), MODE_FULL, I32)


_hp_dot = functools.partial(jnp.dot, preferred_element_type=F32, precision=lax.Precision.HIGHEST)


def _small_body(*refs, kind, normed):
    it = iter(refs)
    x_ref = next(it)
    nw_ref = next(it) if normed else None
    a = x_ref[...]
    if normed:
        a = a * lax.rsqrt(jnp.mean(a * a, axis=-1, keepdims=True) + RMS_EPS) * nw_ref[...]
    if kind == 'swiglu':
        w1_ref, w3_ref, o_ref = it
        u1 = _hp_dot(a, w1_ref[...])
        u3 = _hp_dot(a, w3_ref[...])
        o_ref[...] = u1 * jax.nn.sigmoid(u1) * u3
    elif kind == 'res':
        w_ref, res_ref, o_ref = it
        o_ref[...] = res_ref[...] + _hp_dot(a, w_ref[...])
    else:
        w_ref, o_ref = it
        o_ref[...] = _hp_dot(a, w_ref[...])


def small_matmul(kind, x, ws, e, *, tn, norm_w=None, res=None, n_lo=0, n_cols=None):
    R, K = x.shape
    N = ws[0].shape[-1]
    n_cols = N - n_lo if n_cols is None else n_cols
    nb0 = n_lo // tn
    assert nb0 * tn == n_lo
    normed = norm_w is not None
    row = lambda w: pl.BlockSpec((R, w), lambda n: (0, 0))
    in_specs, args = [row(K)], [x]
    if normed:
        in_specs.append(pl.BlockSpec((1, K), lambda n: (0, 0)))
        args.append(norm_w)
    for w in ws:
        in_specs.append(pl.BlockSpec((None, K, tn), lambda n: (e, 0, nb0 + n)))
        args.append(w)
    ospec = pl.BlockSpec((R, tn), lambda n: (0, n))
    aliases = {}
    if kind == 'res':
        in_specs.append(ospec)
        aliases = {len(args): 0}
        args.append(res)
    return pl.pallas_call(
        functools.partial(_small_body, kind=kind, normed=normed),
        grid=(_cdiv(n_cols, tn),), in_specs=in_specs, out_specs=ospec,
        out_shape=jax.ShapeDtypeStruct((R, n_cols), F32),
        input_output_aliases=aliases,
        compiler_params=_params(("arbitrary",)),
        name="small_matmul_" + kind,
    )(*args)


def _gather_norm_body(src_ref, xp_hbm, xs_hbm, w_ref, o_ref, buf, sem, *, tg, mp):
    def issue(r, c):
        t = src_ref[r]
        dst = buf.at[pl.ds(r, 1), :]
        pl.when(t < mp)(lambda: pltpu.make_async_copy(xp_hbm.at[pl.ds(t, 1), :], dst, sem).start())
        pl.when(t >= mp)(lambda: pltpu.make_async_copy(xs_hbm.at[pl.ds(t - mp, 1), :], dst, sem).start())
        return c

    lax.fori_loop(0, tg, issue, 0, unroll=DMA_UNROLL)

    def wait(r, c):
        pltpu.make_async_copy(xp_hbm.at[pl.ds(0, 1), :], buf.at[pl.ds(r, 1), :], sem).wait()
        return c

    lax.fori_loop(0, tg, wait, 0, unroll=DMA_UNROLL)
    x = buf[...]
    ms = jnp.mean(x * x, axis=-1, keepdims=True)
    o_ref[...] = (x * lax.rsqrt(ms + RMS_EPS) * w_ref[...]).astype(o_ref.dtype)


def gather_rmsnorm_bf16(xp, xs, w, src, tg=256):
    Mp, D = xp.shape
    R = src.shape[0]
    assert R % tg == 0
    return pl.pallas_call(
        functools.partial(_gather_norm_body, tg=tg, mp=Mp),
        grid=(R // tg,),
        in_specs=[pl.BlockSpec((tg,), lambda i: (i,), memory_space=pltpu.SMEM),
                  pl.BlockSpec(memory_space=pl.ANY), pl.BlockSpec(memory_space=pl.ANY),
                  pl.BlockSpec((1, D), lambda i: (0, 0))],
        out_specs=pl.BlockSpec((tg, D), lambda i: (i, 0)),
        out_shape=jax.ShapeDtypeStruct((R, D), BF16),
        scratch_shapes=[pltpu.VMEM((tg, D), F32), pltpu.SemaphoreType.DMA(())],
        compiler_params=_params(("arbitrary",)),
        name="gather_rmsnorm_bf16",
    )(src, xp, xs, w)


def _combine_body(s0_ref, s1_ref, x_ref, y_hbm, o_ref, buf0, buf1, sem, *, tc):
    def issue(r, c):
        pltpu.make_async_copy(y_hbm.at[pl.ds(s0_ref[r], 1), :], buf0.at[pl.ds(r, 1), :], sem).start()
        pltpu.make_async_copy(y_hbm.at[pl.ds(s1_ref[r], 1), :], buf1.at[pl.ds(r, 1), :], sem).start()
        return c

    lax.fori_loop(0, tc, issue, 0, unroll=DMA_UNROLL)

    def wait(r, c):
        pltpu.make_async_copy(y_hbm.at[pl.ds(0, 1), :], buf0.at[pl.ds(r, 1), :], sem).wait()
        pltpu.make_async_copy(y_hbm.at[pl.ds(0, 1), :], buf1.at[pl.ds(r, 1), :], sem).wait()
        return c

    lax.fori_loop(0, tc, wait, 0, unroll=DMA_UNROLL)
    o_ref[...] = x_ref[...] + (buf0[...] + buf1[...])


def combine_top2(x, ys, slot0, slot1, tc=256):
    M, D = x.shape
    nt = _cdiv(M, tc)
    assert slot0.shape[0] == nt * tc
    sspec = pl.BlockSpec((tc,), lambda i: (i,), memory_space=pltpu.SMEM)
    return pl.pallas_call(
        functools.partial(_combine_body, tc=tc),
        grid=(nt,),
        in_specs=[sspec, sspec, pl.BlockSpec((tc, D), lambda i: (i, 0)), pl.BlockSpec(memory_space=pl.ANY)],
        out_specs=pl.BlockSpec((tc, D), lambda i: (i, 0)),
        out_shape=jax.ShapeDtypeStruct((M, D), F32),
        scratch_shapes=[pltpu.VMEM((tc, D), F32), pltpu.VMEM((tc, D), F32), pltpu.SemaphoreType.DMA(())],
        compiler_params=_params(("arbitrary",)),
        name="combine_top2",
    )(slot0, slot1, x, ys)


def _route_tables(idx, gates, tmg, n_tiles):
    M = idx.shape[0]
    e = idx.reshape(-1)
    onehot = (e[:, None] == jnp.arange(N_EXPERTS, dtype=I32)[None, :]).astype(I32)
    csum = jnp.cumsum(onehot, axis=0)
    rank = jnp.take_along_axis(csum, e[:, None], axis=1)[:, 0] - 1
    counts = csum[-1]
    tiles_e = (counts + tmg - 1) // tmg
    tile_end = jnp.cumsum(tiles_e)
    tile_start = tile_end - tiles_e
    slot = tile_start[e] * tmg + rank
    tiles = jnp.arange(n_tiles, dtype=I32)
    te_raw = jnp.sum((tiles[:, None] >= tile_end[None, :]).astype(I32), axis=1)
    total = tile_end[-1]
    valid = tiles < total
    te_last = jnp.sum((total - 1 >= tile_end).astype(I32))
    te = jnp.where(valid, te_raw, te_last).astype(I32)
    tile_rows = jnp.clip(counts[te] - (tiles - tile_start[te]) * tmg, 0, tmg)
    parts = (tile_rows + PART_ROWS - 1) // PART_ROWS
    mode = jnp.where(valid, jnp.where(parts * PART_ROWS >= tmg, MODE_FULL, MODE_PART + parts), MODE_SKIP)
    mode = mode.astype(I32)
    rows = n_tiles * tmg
    src = jnp.zeros((rows,), I32).at[slot].set(jnp.arange(2 * M, dtype=I32) // 2)
    gate_s = jnp.zeros((rows,), F32).at[slot].set(gates.reshape(-1))
    return te, mode, src, gate_s[:, None], slot.reshape(M, TOP_K)


def _rope_tables(pos):
    half = ROPE_DIM // 2
    inv_freq = jnp.exp(-math.log(ROPE_THETA) * 2.0 * jnp.arange(half, dtype=F32) / ROPE_DIM)
    ang = pos.astype(F32)[:, None] * inv_freq[None, :]
    cos, sin = jnp.cos(ang), jnp.sin(ang)
    T = pos.shape[0]
    rest = HEAD_DIM - ROPE_DIM
    ctab = jnp.concatenate([cos, cos, jnp.ones((T, rest), F32)], axis=1)
    stab = jnp.concatenate([-sin, sin, jnp.zeros((T, rest), F32)], axis=1)
    d = jnp.arange(HEAD_DIM)
    partner = jnp.where(d < half, d + half, jnp.where(d < ROPE_DIM, d - half, d))
    swap = (d[:, None] == partner[None, :]).astype(BF16)
    return ctab, stab, swap


def _head_norm_rope(x, nw, ctab, stab, swap):
    y = x * lax.rsqrt(jnp.mean(x * x, axis=-1, keepdims=True) + RMS_EPS) * nw
    return y * ctab + _dot_exact_rhs(y, swap) * stab


def _norm_rope_wide(x, nw, ctab, stab, gmat):
    outs = []
    lane = lax.broadcasted_iota(I32, (x.shape[0], GROUP), 1) % HEAD_DIM
    half = ROPE_DIM // 2
    for q in range(x.shape[1] // GROUP):
        xg = x[:, q * GROUP:(q + 1) * GROUP]
        sq = xg * xg
        hi = sq.astype(BF16)
        lo = (sq - hi.astype(F32)).astype(BF16)
        ss = jnp.dot(hi, gmat, preferred_element_type=F32) + jnp.dot(lo, gmat, preferred_element_type=F32)
        y = xg * lax.rsqrt(ss * (1.0 / HEAD_DIM) + RMS_EPS) * nw
        partner = jnp.where(lane < half, pltpu.roll(y, GROUP - half, axis=1), pltpu.roll(y, half, axis=1))
        outs.append(y * ctab + partner * stab)
    return outs


def _attn_prompt_body(q_ref, kc_ref, kp_ref, vc_ref, vp_ref, cc_ref, sc_ref, cp_ref, sp_ref, qn_ref, kn_ref,
                      sink_ref, g_ref, o_ref, knew_ref):
    i = pl.program_id(1)
    gmat = g_ref[...]
    cc, sc, cp, sp = cc_ref[...], sc_ref[...], cp_ref[...], sp_ref[...]
    qn, kn = qn_ref[...], kn_ref[...]
    R = Q_PER_KV * BLOCK
    row = lax.broadcasted_iota(I32, (R, 2 * BLOCK), 0)
    col = lax.broadcasted_iota(I32, (R, 2 * BLOCK), 1)
    rel = (row % BLOCK) - col + BLOCK
    mask = (rel >= 0) & (rel <= WINDOW) & jnp.logical_not((i == 0) & (col < BLOCK))
    rowh = lax.broadcasted_iota(I32, (R, 1), 0) // BLOCK
    headmask = (lax.broadcasted_iota(I32, (R, GROUP), 0) // BLOCK
                == lax.broadcasted_iota(I32, (R, GROUP), 1) // HEAD_DIM)
    lane_lo = lax.broadcasted_iota(I32, (2 * BLOCK, LANES), 1) < HEAD_DIM
    heads_per_group = GROUP // HEAD_DIM

    q_rot = _norm_rope_wide(q_ref[...], qn, cc, sc, gmat)
    k_cur = _norm_rope_wide(kc_ref[...], kn, cc, sc, gmat)
    k_prev = _norm_rope_wide(kp_ref[...], kn, cp, sp, gmat)
    for j in range(KV_WIDTH // GROUP):
        knew_ref[:, j * GROUP:(j + 1) * GROUP] = k_cur[j]

    def spread(x, g):
        other = pltpu.roll(x, HEAD_DIM, axis=1)
        one = jnp.where(lane_lo, x, other) if g % 2 == 0 else jnp.where(lane_lo, other, x)
        return jnp.concatenate([one, one], axis=1)

    for g in range(N_KV_HEADS):
        j, m = g // heads_per_group, (g % heads_per_group) // 2
        lanes = slice(m * LANES, (m + 1) * LANES)
        kcat = jnp.concatenate([k_prev[j][:, lanes], k_cur[j][:, lanes]], axis=0)
        c0 = g // 2 * LANES
        vcat = jnp.concatenate([vp_ref[:, c0:c0 + LANES], vc_ref[:, c0:c0 + LANES]], axis=0)
        kx = spread(kcat, g).astype(BF16)
        vx = spread(vcat, g).astype(BF16)
        qbd = jnp.where(headmask, jnp.concatenate([q_rot[g]] * Q_PER_KV, axis=0), 0.0).astype(BF16)
        s = lax.dot_general(qbd, kx, (((1,), (1,)), ((), ())), preferred_element_type=F32) * ATT_SCALE
        s = jnp.where(mask, s, -jnp.inf)
        sink = jnp.zeros((R, 1), F32)
        for h in range(Q_PER_KV):
            sink = jnp.where(rowh == h, sink_ref[g * Q_PER_KV + h], sink)
        mx = jnp.maximum(jnp.max(s, axis=-1, keepdims=True), sink)
        p = jnp.exp(s - mx)
        denom = jnp.sum(p, axis=-1, keepdims=True) + jnp.exp(sink - mx)
        o = jnp.dot((p / denom).astype(BF16), vx, preferred_element_type=F32)
        o = jnp.where(headmask, o, 0.0)
        out = o[0:BLOCK]
        for h in range(1, Q_PER_KV):
            out = out + o[h * BLOCK:(h + 1) * BLOCK]
        o_ref[:, g * GROUP:(g + 1) * GROUP] = out.astype(o_ref.dtype)


def attention_prompt(u_att, B, T, q_norm, k_norm, sinks):
    nb = T // BLOCK
    reps = GROUP // HEAD_DIM
    ctab, stab, _ = _rope_tables(jnp.arange(T, dtype=I32))
    ctab, stab = jnp.tile(ctab, (1, reps)), jnp.tile(stab, (1, reps))
    kcol, vcol = ATT_WIDTH // KV_WIDTH, ATT_WIDTH // KV_WIDTH + 1
    cur = lambda b, i: b * nb + i
    prev = lambda b, i: b * nb + jnp.maximum(i - 1, 0)
    tspec_c = pl.BlockSpec((BLOCK, GROUP), lambda b, i: (i, 0))
    tspec_p = pl.BlockSpec((BLOCK, GROUP), lambda b, i: (jnp.maximum(i - 1, 0), 0))
    wspec = pl.BlockSpec((1, GROUP), lambda b, i: (0, 0))
    return pl.pallas_call(
        _attn_prompt_body,
        grid=(B, nb),
        in_specs=[pl.BlockSpec((BLOCK, ATT_WIDTH), lambda b, i: (cur(b, i), 0)),
                  pl.BlockSpec((BLOCK, KV_WIDTH), lambda b, i: (cur(b, i), kcol)),
                  pl.BlockSpec((BLOCK, KV_WIDTH), lambda b, i: (prev(b, i), kcol)),
                  pl.BlockSpec((BLOCK, KV_WIDTH), lambda b, i: (cur(b, i), vcol)),
                  pl.BlockSpec((BLOCK, KV_WIDTH), lambda b, i: (prev(b, i), vcol)),
                  tspec_c, tspec_c, tspec_p, tspec_p, wspec, wspec,
                  pl.BlockSpec(memory_space=pltpu.SMEM),
                  pl.BlockSpec((GROUP, GROUP), lambda b, i: (0, 0))],
        out_specs=[pl.BlockSpec((BLOCK, ATT_WIDTH), lambda b, i: (cur(b, i), 0)),
                   pl.BlockSpec((BLOCK, KV_WIDTH), lambda b, i: (b, 0))],
        out_shape=[jax.ShapeDtypeStruct((B * T, ATT_WIDTH), BF16),
                   jax.ShapeDtypeStruct((B * BLOCK, KV_WIDTH), F32)],
        compiler_params=_params(("arbitrary", "arbitrary")),
        name="attention_prompt",
    )(u_att, u_att, u_att, u_att, u_att, ctab, stab, ctab, stab, jnp.tile(q_norm, reps)[None, :],
      jnp.tile(k_norm, reps)[None, :], sinks, _head_sum_matrix())


def _attn_sample_body(u_ref, k_ref, v_ref, ct_ref, st_ref, qn_ref, kn_ref, sink_ref, swap_ref, rexp_ref, bmask_ref,
                      o_ref, newk_ref, newv_ref, q_s, kx_s, vx_s, kn_s):
    b = pl.program_id(0)
    rexp = rexp_ref[...]

    @pl.when(b == 0)
    def _():
        swap = swap_ref[...]
        ct, st = ct_ref[...], st_ref[...]
        for h in range(N_Q_HEADS):
            sl = slice(h * HEAD_DIM, (h + 1) * HEAD_DIM)
            q_s[:, sl] = _head_norm_rope(u_ref[:, sl], qn_ref[...], ct, st, swap)
        for g in range(N_KV_HEADS):
            sl = slice(g * HEAD_DIM, (g + 1) * HEAD_DIM)
            kn_s[:, sl] = _head_norm_rope(u_ref[:, ATT_WIDTH + g * HEAD_DIM:ATT_WIDTH + (g + 1) * HEAD_DIM],
                                          kn_ref[...], ct, st, swap)
        kx_s[...] = _dot_exact_rhs(kn_s[...], rexp)
        vx_s[...] = _dot_exact_rhs(u_ref[:, ATT_WIDTH + KV_WIDTH:], rexp)

    bmask = bmask_ref[...]
    qfull = q_s[pl.ds(b, 1), :] * bmask
    kb, vb = k_ref[...], v_ref[...]
    kexp = _dot_exact_rhs(kb, rexp)
    vexp = _dot_exact_rhs(vb, rexp)
    s = lax.dot_general(qfull, kexp, (((1,), (1,)), ((), ())), preferred_element_type=F32,
                        precision=lax.Precision.HIGHEST) * ATT_SCALE
    s_self = jnp.sum(qfull * kx_s[pl.ds(b, 1), :], axis=-1, keepdims=True) * ATT_SCALE
    sink = sink_ref[...]
    m = jnp.maximum(jnp.maximum(jnp.max(s, axis=-1, keepdims=True), s_self), sink)
    p = jnp.exp(s - m)
    p_self = jnp.exp(s_self - m)
    denom = jnp.sum(p, axis=-1, keepdims=True) + p_self + jnp.exp(sink - m)
    o = _hp_dot(p / denom, vexp)
    o = o + (p_self / denom) * vx_s[pl.ds(b, 1), :]
    o_ref[pl.ds(b, 1), :] = jnp.sum(o * bmask, axis=0, keepdims=True)
    last = lax.broadcasted_iota(I32, kb.shape, 0) == WINDOW - 1
    newk_ref[...] = jnp.where(last, kn_s[pl.ds(b, 1), :], pltpu.roll(kb, WINDOW - 1, axis=0))
    newv_ref[...] = jnp.where(last, u_ref[pl.ds(b, 1), ATT_WIDTH + KV_WIDTH:], pltpu.roll(vb, WINDOW - 1, axis=0))


def attention_sample(u_att, row0, Bs, cache_k, cache_v, layer, q_norm, k_norm, sinks):
    assert row0 % Bs == 0
    ctab, stab, swap = _rope_tables(jnp.full((1,), PAST_LEN, I32))
    gd = jnp.arange(KV_WIDTH)
    hd = jnp.arange(ATT_WIDTH)
    rexp = ((gd[:, None] // HEAD_DIM == hd[None, :] // (HEAD_DIM * Q_PER_KV))
            & (gd[:, None] % HEAD_DIM == hd[None, :] % HEAD_DIM)).astype(BF16)
    bmask = (jnp.arange(N_Q_HEADS)[:, None] == hd[None, :] // HEAD_DIM).astype(F32)
    full = lambda shape: pl.BlockSpec(shape, lambda b: (0,) * len(shape))
    cspec = pl.BlockSpec((None, None, WINDOW, KV_WIDTH), lambda b: (layer, b, 0, 0))
    ospec = pl.BlockSpec((None, WINDOW, KV_WIDTH), lambda b: (b, 0, 0))
    return pl.pallas_call(
        _attn_sample_body,
        grid=(Bs,),
        in_specs=[pl.BlockSpec((Bs, ATT_PROJ), lambda b: (row0 // Bs, 0)), cspec, cspec,
                  full((1, HEAD_DIM)), full((1, HEAD_DIM)), full((1, HEAD_DIM)), full((1, HEAD_DIM)),
                  full((N_Q_HEADS, 1)), full((HEAD_DIM, HEAD_DIM)), full((KV_WIDTH, ATT_WIDTH)),
                  full((N_Q_HEADS, ATT_WIDTH))],
        out_specs=[full((Bs, ATT_WIDTH)), ospec, ospec],
        out_shape=[jax.ShapeDtypeStruct((Bs, ATT_WIDTH), F32),
                   jax.ShapeDtypeStruct((Bs, WINDOW, KV_WIDTH), F32),
                   jax.ShapeDtypeStruct((Bs, WINDOW, KV_WIDTH), F32)],
        scratch_shapes=[pltpu.VMEM((Bs, ATT_WIDTH), F32), pltpu.VMEM((Bs, ATT_WIDTH), F32),
                        pltpu.VMEM((Bs, ATT_WIDTH), F32), pltpu.VMEM((Bs, KV_WIDTH), F32)],
        compiler_params=_params(("arbitrary",)),
        name="attention_sample",
    )(u_att, cache_k, cache_v, ctab, stab, q_norm[None, :], k_norm[None, :], sinks[:, None], swap, rexp, bmask)


def _head_sum_matrix():
    i = jnp.arange(GROUP)
    return (i[:, None] // RWKV_HEAD == i[None, :] // RWKV_HEAD).astype(BF16)


def _head_sum(x, gmat):
    return jnp.concatenate([_dot_exact_rhs(x[:, q * GROUP:(q + 1) * GROUP], gmat) for q in range(N_GROUPS)], axis=1)


def _rwkv_prep_body(*refs, shifted, has_vres):
    it = iter(refs)
    u_ref = next(it)
    prev_ref = next(it)
    mu_ref, w0_ref, a0_ref, kk_ref, ka_ref, rk_ref, lora_ref, g_ref = (next(it) for _ in range(8))
    if has_vres:
        v0_ref, v1_ref, v2_ref, vf_ref = (next(it) for _ in range(4))
    r_o, w_o, k_o, v_o, a_o, b_o, g_o, bonus_o, tail_s = it
    u = u_ref[...]
    if shifted:
        first = jnp.where(pl.program_id(1) > 0, prev_ref[SUBLANES - 1:SUBLANES, :], 0.0)
        prev = jnp.where(lax.broadcasted_iota(I32, u.shape, 0) == 0, first, pltpu.roll(u, 1, axis=0))
    else:
        prev = prev_ref[...]
    xs = u + (prev - u) * mu_ref[...]
    r = xs[:, :RWKV_WIDTH]
    k = xs[:, RWKV_WIDTH:2 * RWKV_WIDTH]
    v = xs[:, 2 * RWKV_WIDTH:3 * RWKV_WIDTH]
    tail_s[...] = jnp.zeros(tail_s.shape, F32)
    tail_s[:, :LORA_IN] = xs[:, 3 * RWKV_WIDTH:]
    t = tail_s[...]
    def dot(x, w):
        if w.dtype == F32:
            return _hp_dot(x, w)
        return jnp.dot(x.astype(BF16), w, preferred_element_type=F32)

    dw = dot(jnp.tanh(t), lora_ref[0])
    da = dot(t, lora_ref[1])
    g = dot(jax.nn.sigmoid(t), lora_ref[2])
    z = -(w0_ref[...] + dw)
    softplus = jnp.maximum(z, 0.0) + jnp.log(1.0 + jnp.exp(-jnp.abs(z)))
    w_log = -softplus - 0.5
    decay = jnp.exp(-jnp.exp(w_log))
    a = jax.nn.sigmoid(a0_ref[...] + da)
    if has_vres:
        lo = dot(dot(v, v1_ref[...]), v2_ref[...])
        v = v + (vf_ref[...] - v) * jax.nn.sigmoid(v0_ref[...] + lo)
    gmat = g_ref[...]
    kk = k * kk_ref[...]
    kk = kk / jnp.maximum(jnp.sqrt(_head_sum(kk * kk, gmat)), 1e-12)
    k = k * (1.0 + (a - 1.0) * ka_ref[...])
    r_o[...] = r
    w_o[...] = decay
    k_o[...] = k
    v_o[...] = v
    a_o[...] = -kk
    b_o[...] = kk * a
    g_o[...] = g
    bonus_o[...] = _head_sum(r * k * rk_ref[...], gmat) * v


def rwkv_prep(u_rw, row0, B, T, prev_rows, p, v_first, tr):
    n = B * T
    shifted = prev_rows is None
    if shifted:
        assert T % tr == 0 and row0 == 0
        nt = T // tr
        grid = (B, nt)
        rowblk = lambda b, i: b * nt + i
        prev_spec = pl.BlockSpec((SUBLANES, SHIFT_WIDTH),
                                 lambda b, i: (jnp.maximum(rowblk(b, i) * (tr // SUBLANES) - 1, 0), 0))
        prev_arr = u_rw
        oblk = lambda b, i: (b * nt + i, 0)
    else:
        assert T == 1 and tr == B and row0 % tr == 0
        grid = (1, 1)
        rowblk = lambda b, i: row0 // tr
        prev_spec = pl.BlockSpec((tr, SHIFT_WIDTH), lambda b, i: (0, 0))
        prev_arr = prev_rows
        oblk = lambda b, i: (0, 0)
    vec = lambda w: pl.BlockSpec((1, w), lambda b, i: (0, 0))
    in_specs = [pl.BlockSpec((tr, SHIFT_WIDTH), lambda b, i: (rowblk(b, i), 0)), prev_spec,
                vec(SHIFT_WIDTH), vec(RWKV_WIDTH), vec(RWKV_WIDTH), vec(RWKV_WIDTH), vec(RWKV_WIDTH),
                vec(RWKV_WIDTH), pl.BlockSpec((3, LORA_PAD, RWKV_WIDTH), lambda b, i: (0, 0, 0)),
                pl.BlockSpec((GROUP, GROUP), lambda b, i: (0, 0))]
    args = [u_rw, prev_arr, p['mu'], p['w0'], p['a0'], p['k_k'], p['k_a'], p['r_k'], p['lora'], _head_sum_matrix()]
    has_vres = v_first is not None
    if has_vres:
        in_specs += [vec(RWKV_WIDTH), pl.BlockSpec((RWKV_WIDTH, MV_LORA), lambda b, i: (0, 0)),
                     pl.BlockSpec((MV_LORA, RWKV_WIDTH), lambda b, i: (0, 0)),
                     pl.BlockSpec((tr, RWKV_WIDTH), oblk)]
        args += [p['v0'], p['v1'], p['v2'], v_first]
    ospec = pl.BlockSpec((tr, RWKV_WIDTH), oblk)
    return pl.pallas_call(
        functools.partial(_rwkv_prep_body, shifted=shifted, has_vres=has_vres),
        grid=grid, in_specs=in_specs, out_specs=[ospec] * 8,
        out_shape=[jax.ShapeDtypeStruct((n, RWKV_WIDTH), F32)] * 8,
        scratch_shapes=[pltpu.VMEM((tr, LORA_PAD), F32)],
        compiler_params=_params(("arbitrary", "arbitrary")),
        name="rwkv_prep",
    )(*args)


def _rwkv_params(l, shift_mu, w0, w2, a0, a2, g2, k_k, k_a, r_k, v0, v1, v2):
    lora = jnp.zeros((3, LORA_PAD, RWKV_WIDTH), F32)
    lora = lora.at[0, :DECAY_LORA].set(w2[l])
    lora = lora.at[1, DECAY_LORA:DECAY_LORA + AAA_LORA].set(a2[l])
    lora = lora.at[2, DECAY_LORA + AAA_LORA:LORA_IN].set(g2[l])
    p = {'mu': shift_mu[l][None, :], 'w0': w0[l][None, :], 'a0': a0[l][None, :], 'k_k': k_k[l][None, :],
         'k_a': k_a[l][None, :], 'r_k': r_k[l].reshape(1, RWKV_WIDTH), 'lora': lora}
    if l > 0:
        p['v0'] = v0[l - 1][None, :]
        p['v1'] = v1[l - 1]
        p['v2'] = v2[l - 1]
    p_bf16 = {k_: (v_.astype(BF16) if k_ in ('lora', 'v1', 'v2') else v_) for k_, v_ in p.items()}
    return p, p_bf16


def _wkv_body(r_ref, w_ref, k_ref, v_ref, a_ref, b_ref, s0_ref, g_ref, eye_ref, y_ref, sout_ref, s_s, sb_s, *, nb,
              tc, precise):
    c = pl.program_id(1)

    @pl.when(c == 0)
    def _():
        s_s[...] = s0_ref[...]
        sb_s[...] = s0_ref[...].astype(BF16)

    gmat = g_ref[...]
    eye = eye_ref[...][None]
    eye_b = eye.astype(BF16)
    rows = nb * RWKV_HEAD
    dot = functools.partial(jnp.dot, preferred_element_type=F32)

    groups = [slice(q * GROUP, (q + 1) * GROUP) for q in range(N_GROUPS)]

    def block(red, n):
        return red[n * rows:(n + 1) * rows].reshape(nb, RWKV_HEAD, GROUP)

    def diagonal(yx):
        d = (yx * eye).reshape(nb, RWKV_HEAD // SUBLANES, SUBLANES, GROUP)
        return jnp.sum(jnp.sum(d, axis=1), axis=1, keepdims=True)

    def store_y(t, yx):
        for q, sl in enumerate(groups):
            y_ref[:, pl.ds(t, 1), sl] = diagonal(block(yx, q))

    def step_bf16(t, carry):
        tp = jnp.maximum(t - 1, 0)
        w_t, k_t, v_t, a_t, b_t = (ref[:, pl.ds(t, 1), :] for ref in (w_ref, k_ref, v_ref, a_ref, b_ref))
        a_b, v_b, r_b = a_t.astype(BF16), v_t.astype(BF16), r_ref[:, pl.ds(tp, 1), :].astype(BF16)
        tiles = (nb, RWKV_HEAD // BF16_ROWS, BF16_ROWS, GROUP)
        lhs = []
        for q, sl in enumerate(groups):
            sb = sb_s[q]
            parts = [(sb * a_b[:, :, sl]).reshape(tiles), (eye_b * v_b[:, :, sl]).reshape(tiles),
                     (sb * r_b[:, :, sl]).reshape(tiles)]
            lhs.append(jnp.stack(parts, axis=2).reshape(3 * rows, GROUP))
        red = dot(jnp.concatenate(lhs, axis=0), gmat)
        ys = []
        for q, sl in enumerate(groups):
            rq = red[3 * q * rows:3 * (q + 1) * rows].reshape(tiles[:2] + (3,) + tiles[2:])
            sa, vx = (rq[:, :, j].reshape(nb, RWKV_HEAD, GROUP) for j in range(2))
            s_new = s_s[q] * w_t[:, :, sl] + sa * b_t[:, :, sl] + vx * k_t[:, :, sl]
            s_s[q] = s_new
            sb_s[q] = s_new.astype(BF16)
            ys.append(diagonal(rq[:, :, 2].reshape(nb, RWKV_HEAD, GROUP)))

        @pl.when(t > 0)
        def _():
            for q, sl in enumerate(groups):
                y_ref[:, pl.ds(tp, 1), sl] = ys[q]

        return carry

    def last_y_bf16():
        r_b = r_ref[:, pl.ds(tc - 1, 1), :].astype(BF16)
        lhs = [(sb_s[q] * r_b[:, :, sl]).reshape(rows, GROUP) for q, sl in enumerate(groups)]
        store_y(tc - 1, dot(jnp.concatenate(lhs, axis=0), gmat))

    def head_sums3(xs):
        lhs = []
        for x in xs:
            lhs += list(_split3(x.reshape(rows, GROUP)))
        red = dot(jnp.concatenate(lhs, axis=0), gmat)
        out = []
        for q in range(len(xs)):
            o = 3 * q * rows
            acc = red[o:o + rows] + red[o + rows:o + 2 * rows] + red[o + 2 * rows:o + 3 * rows]
            out.append(acc.reshape(nb, RWKV_HEAD, GROUP))
        return out

    def step_f32(t, carry):
        r_t, w_t, k_t, v_t, a_t, b_t = (ref[:, pl.ds(t, 1), :] for ref in (r_ref, w_ref, k_ref, v_ref, a_ref, b_ref))
        groups = [slice(q * GROUP, (q + 1) * GROUP) for q in range(N_GROUPS)]
        sa = head_sums3([s_s[q] * a_t[:, :, sl] for q, sl in enumerate(groups)])
        vx = head_sums3([eye * v_t[:, :, sl] for sl in groups])
        prods = []
        for q, sl in enumerate(groups):
            s_new = s_s[q] * w_t[:, :, sl] + sa[q] * b_t[:, :, sl] + vx[q] * k_t[:, :, sl]
            s_s[q] = s_new
            prods.append(s_new * r_t[:, :, sl])
        yx = head_sums3(prods)
        for q, sl in enumerate(groups):
            y_ref[:, pl.ds(t, 1), sl] = jnp.sum(yx[q] * eye, axis=1, keepdims=True)
        return carry

    if precise:
        lax.fori_loop(0, tc, step_f32, 0)
    else:
        lax.fori_loop(0, tc, step_bf16, 0)
        last_y_bf16()

    @pl.when(c == pl.num_programs(1) - 1)
    def _():
        sout_ref[...] = s_s[...]


def wkv_scan(r, w, k, v, a, b, s0, nb=4, tc=64, precise=False):
    B, T, W = r.shape
    tc = min(tc, T)
    assert B % nb == 0 and T % tc == 0
    xspec = pl.BlockSpec((nb, tc, W), lambda i, c: (i, c, 0))
    sspec = pl.BlockSpec((N_GROUPS, nb, RWKV_HEAD, GROUP), lambda i, c: (0, i, 0, 0))
    j = jnp.arange(GROUP)
    eye = (jnp.arange(RWKV_HEAD)[:, None] == j[None, :] % RWKV_HEAD).astype(F32)
    return pl.pallas_call(
        functools.partial(_wkv_body, nb=nb, tc=tc, precise=precise),
        grid=(B // nb, T // tc),
        in_specs=[xspec] * 6 + [sspec, pl.BlockSpec((GROUP, GROUP), lambda i, c: (0, 0)),
                                pl.BlockSpec((RWKV_HEAD, GROUP), lambda i, c: (0, 0))],
        out_specs=[xspec, sspec],
        out_shape=[jax.ShapeDtypeStruct((B, T, W), F32), jax.ShapeDtypeStruct(s0.shape, F32)],
        scratch_shapes=[pltpu.VMEM((N_GROUPS, nb, RWKV_HEAD, GROUP), F32),
                        pltpu.VMEM((N_GROUPS, nb, RWKV_HEAD, GROUP), BF16)],
        compiler_params=_params(("arbitrary", "arbitrary")),
        name="wkv_scan",
    )(r, w, k, v, a, b, s0, _head_sum_matrix(), eye)


def _state_to_groups(s):
    B = s.shape[0]
    return s.reshape(B, N_GROUPS, HEADS_PER_GROUP, RWKV_HEAD, RWKV_HEAD).transpose(1, 0, 3, 2, 4).reshape(
        N_GROUPS, B, RWKV_HEAD, GROUP)


def _state_from_groups(s):
    B = s.shape[1]
    return s.reshape(N_GROUPS, B, RWKV_HEAD, HEADS_PER_GROUP, RWKV_HEAD).transpose(1, 0, 3, 2, 4).reshape(
        B, N_RWKV_HEADS, RWKV_HEAD, RWKV_HEAD)


def _rwkv_post_body(y_ref, bonus_ref, g_ref, lw_ref, lb_ref, gm_ref, o_ref):
    gmat = gm_ref[...]
    y = y_ref[...]
    mean = _head_sum(y, gmat) * (1.0 / RWKV_HEAD)
    d = y - mean
    var = _head_sum(d * d, gmat) * (1.0 / RWKV_HEAD)
    yn = d * lax.rsqrt(var + GN_EPS) * lw_ref[...] + lb_ref[...]
    o_ref[...] = ((yn + bonus_ref[...]) * g_ref[...]).astype(o_ref.dtype)


def rwkv_post(y, bonus, g, ln_w, ln_b, out_dtype, tr=256):
    n, W = y.shape
    tr = min(tr, n)
    assert n % tr == 0
    spec = pl.BlockSpec((tr, W), lambda i: (i, 0))
    vec = pl.BlockSpec((1, W), lambda i: (0, 0))
    return pl.pallas_call(
        _rwkv_post_body, grid=(n // tr,),
        in_specs=[spec, spec, spec, vec, vec, pl.BlockSpec((GROUP, GROUP), lambda i: (0, 0))],
        out_specs=spec, out_shape=jax.ShapeDtypeStruct((n, W), out_dtype),
        compiler_params=_params(("parallel",)),
        name="rwkv_post",
    )(y, bonus, g, ln_w[None, :], ln_b[None, :], _head_sum_matrix())


def _rwkv_mixer(u_rw, B, T, prev_rows, s0_groups, p, v_first, ln_w, ln_b, tr, precise):
    r, w, k, v, a, b, g, bonus = rwkv_prep(u_rw, 0, B, T, prev_rows, p, v_first, tr)
    sh = lambda t: t.reshape(B, T, RWKV_WIDTH)
    y, s_out = wkv_scan(sh(r), sh(w), sh(k), sh(v), sh(a), sh(b), s0_groups, precise=precise)
    out = rwkv_post(y.reshape(B * T, RWKV_WIDTH), bonus, g, ln_w, ln_b, F32 if precise else BF16)
    return out, _state_from_groups(s_out), v


TM = 1024
TM_MOE = 512
TC = 256
TR_PREP = 128
TN_SMALL = 512


def kernel(x_prompt, x_sample, state_shift, state_wkv, cache_win_k, cache_win_v, attn_norm_w, w_in, q_norm_w,
           k_norm_w, attn_sinks, shift_mu, w0, w2, a0, a2, g2, k_k, k_a, r_k, ln_x_w, ln_x_b, v0, v1, v2, w_out,
           ffn_norm_w, dense_w1, dense_w3, dense_w2, router_w, moe_w1, moe_w3, moe_w2):
    Bp, Tp, D = x_prompt.shape
    Bs, Ts, _ = x_sample.shape
    assert Ts == 1 and cache_win_k.shape[2] == WINDOW and Tp % BLOCK == 0
    Mp, Ms = Bp * Tp, Bs * Ts
    assert Mp % TM == 0 and Mp % TC == 0
    xp = x_prompt.reshape(Mp, D)
    xs = x_sample.reshape(Ms, D)
    cache_k = cache_win_k.reshape(DEPTH, Bs, WINDOW, KV_WIDTH)
    cache_v = cache_win_v.reshape(DEPTH, Bs, WINDOW, KV_WIDTH)
    n_moe_tiles = (TOP_K * (Mp + Ms)) // TM_MOE + N_EXPERTS
    zero_state = jnp.zeros((N_GROUPS, Bp, RWKV_HEAD, GROUP), F32)
    vf_p = vf_s = None
    outs = {k_: [] for k_ in ('p_row', 'p_wkv', 'p_k', 'p_v', 's_row', 's_wkv', 's_k', 's_v')}
    for l in range(DEPTH):
        te, mode = _dense_tiles(Mp, TM, l)
        p_f32, p_bf16 = _rwkv_params(l, shift_mu, w0, w2, a0, a2, g2, k_k, k_a, r_k, v0, v1, v2)
        an = attn_norm_w[l][None, :]
        h = rmsnorm_bf16(xp, an)
        u_att = matmul_proj(h, w_in, te, mode, n_lo=0, n_cols=ATT_PROJ, tm=TM, tn=512)
        u_rw = matmul_proj(h, w_in, te, mode, n_lo=ATT_PROJ, n_cols=SHIFT_WIDTH, tm=TM, tn=512)
        att_p, knew_p = attention_prompt(u_att, Bp, Tp, q_norm_w[l], k_norm_w[l], attn_sinks[l])
        rw_p, S_p, v_p_first = _rwkv_mixer(u_rw, Bp, Tp, None, zero_state, p_bf16, vf_p, ln_x_w[l], ln_x_b[l],
                                           TR_PREP, False)
        xp = matmul_residual(jnp.concatenate([att_p, rw_p], axis=1), w_out, xp, te, mode, tm=TM, tn=512, tk=D)
        us_att = small_matmul('proj', xs, [w_in], l, tn=TN_SMALL, norm_w=an, n_lo=0, n_cols=ATT_PROJ)
        us_rw = small_matmul('proj', xs, [w_in], l, tn=TN_SMALL, norm_w=an, n_lo=ATT_PROJ, n_cols=SHIFT_WIDTH)
        att_s, k_s, v_s = attention_sample(us_att, 0, Bs, cache_k, cache_v, l, q_norm_w[l], k_norm_w[l],
                                           attn_sinks[l])
        rw_s, S_s, v_s_first = _rwkv_mixer(us_rw, Bs, Ts, state_shift[l], _state_to_groups(state_wkv[l]), p_f32,
                                           vf_s, ln_x_w[l], ln_x_b[l], Bs, True)
        xs = small_matmul('res', jnp.concatenate([att_s, rw_s], axis=1), [w_out], l, tn=TN_SMALL, res=xs)
        if l == 0:
            vf_p, vf_s = v_p_first, v_s_first
        fn = ffn_norm_w[l][None, :]
        if l % 2 == 0:
            te, mode = _dense_tiles(Mp, TM, l // 2)
            h = rmsnorm_bf16(xp, fn)
            g = matmul_swiglu_in(h, dense_w1, dense_w3, te, mode, tm=TM, tn=256)
            xp = matmul_residual(g, dense_w2, xp, te, mode, tm=TM, tn=512, tk=D_FF // 4)
            gs = small_matmul('swiglu', xs, [dense_w1, dense_w3], l // 2, tn=TN_SMALL // 2, norm_w=fn)
            xs = small_matmul('res', gs, [dense_w2], l // 2, tn=TN_SMALL // 4, res=xs)
        else:
            router_pad = jnp.pad(router_w[l // 2], ((0, 0), (0, LANES - N_EXPERTS)))
            idx_p, gates_p = route_top2(xp, fn, router_pad)
            idx_s, gates_s = route_top2(xs, fn, router_pad, tr=Ms)
            te, mode, src, gate_rows, slot = _route_tables(
                jnp.concatenate([idx_p, idx_s], axis=0), jnp.concatenate([gates_p, gates_s], axis=0),
                TM_MOE, n_moe_tiles)
            rows = gather_rmsnorm_bf16(xp, xs, fn, src, tg=TC)
            g = matmul_swiglu_in(rows, moe_w1[l // 2], moe_w3[l // 2], te, mode, tm=TM_MOE, tn=512, parts=True)
            ys = matmul_residual(g, moe_w2[l // 2], jnp.zeros((rows.shape[0], D), F32), te, mode,
                                 tm=TM_MOE, tn=1024, tk=D_FF // 4, parts=True, scale=gate_rows)
            xp = combine_top2(xp, ys, slot[:Mp, 0], slot[:Mp, 1], tc=TC)
            xs = combine_top2(xs, ys, slot[Mp:, 0], slot[Mp:, 1], tc=Ms)
        u_att_p = u_att.reshape(Bp, Tp, ATT_PROJ)
        outs['p_row'].append(u_rw.reshape(Bp, Tp, SHIFT_WIDTH)[:, Tp - 1])
        outs['p_wkv'].append(S_p)
        outs['p_k'].append(knew_p.reshape(Bp, WINDOW, N_KV_HEADS, HEAD_DIM))
        outs['p_v'].append(u_att_p[:, Tp - WINDOW:, ATT_WIDTH + KV_WIDTH:].reshape(Bp, WINDOW, N_KV_HEADS, HEAD_DIM))
        outs['s_row'].append(us_rw)
        outs['s_wkv'].append(S_s)
        outs['s_k'].append(k_s.reshape(Bs, WINDOW, N_KV_HEADS, HEAD_DIM))
        outs['s_v'].append(v_s.reshape(Bs, WINDOW, N_KV_HEADS, HEAD_DIM))
    return (xp.reshape(Bp, Tp, D), xs.reshape(Bs, Ts, D),
            jnp.stack(outs['p_row']), jnp.stack(outs['p_wkv']), jnp.stack(outs['p_k']), jnp.stack(outs['p_v']),
            jnp.stack(outs['s_row']), jnp.stack(outs['s_wkv']), jnp.stack(outs['s_k']), jnp.stack(outs['s_v']))
```

```python
import functools
import math

import jax
import jax.numpy as jnp
from jax import lax
from jax.experimental import pallas as pl
from jax.experimental.pallas import tpu as pltpu

F32 = jnp.float32
BF16 = jnp.bfloat16
I32 = jnp.int32

D_MODEL = 4096
DEPTH = 2
PAST_LEN = 16384
HEAD_DIM = 64
ATT_WIDTH = D_MODEL // 2
N_Q_HEADS = ATT_WIDTH // HEAD_DIM
N_KV_HEADS = 8
Q_PER_KV = N_Q_HEADS // N_KV_HEADS
KV_WIDTH = N_KV_HEADS * HEAD_DIM
ATT_PROJ = ATT_WIDTH + 2 * KV_WIDTH
WINDOW = 128
BLOCK = 128
ROPE_DIM = HEAD_DIM // 4
ROPE_THETA = 500000.0
ATT_SCALE = HEAD_DIM ** -0.5
RWKV_WIDTH = D_MODEL - ATT_WIDTH
RWKV_HEAD = 64
N_RWKV_HEADS = RWKV_WIDTH // RWKV_HEAD
DECAY_LORA = 96
AAA_LORA = 96
GATE_LORA = 256
MV_LORA = 64
LORA_IN = DECAY_LORA + AAA_LORA + GATE_LORA
SHIFT_WIDTH = 3 * RWKV_WIDTH + LORA_IN
D_FF = 14336
N_EXPERTS = 8
TOP_K = 2
RMS_EPS = 1e-5
GN_EPS = 64e-5

VMEM_LIMIT_BYTES = 58 * 2 ** 20
LANES = 128
SUBLANES = 8
BF16_ROWS = 2 * SUBLANES
LORA_PAD = 4 * LANES
GROUP = 2 * LANES
N_GROUPS = RWKV_WIDTH // GROUP
HEADS_PER_GROUP = GROUP // RWKV_HEAD

MODE_SKIP, MODE_FULL = 0, 1
MODE_PART = 8
PART_ROWS = LANES
DMA_UNROLL = 8


def _cdiv(a, b):
    return -(-a // b)


def _params(sem):
    return pltpu.CompilerParams(dimension_semantics=sem, vmem_limit_bytes=VMEM_LIMIT_BYTES)


def _split3(x):
    hi = x.astype(BF16)
    r1 = x - hi.astype(F32)
    mid = r1.astype(BF16)
    lo = (r1 - mid.astype(F32)).astype(BF16)
    return hi, mid, lo


def _dot_exact_rhs(x, rhs_bf16):
    hi, mid, lo = _split3(x)
    d = functools.partial(jnp.dot, preferred_element_type=F32)
    return d(hi, rhs_bf16) + d(mid, rhs_bf16) + d(lo, rhs_bf16)


def _rmsnorm_body(x_ref, w_ref, o_ref):
    x = x_ref[...]
    ms = jnp.mean(x * x, axis=-1, keepdims=True)
    o_ref[...] = (x * lax.rsqrt(ms + RMS_EPS) * w_ref[...]).astype(o_ref.dtype)


def rmsnorm_bf16(x, w, tr=512):
    M, D = x.shape
    return pl.pallas_call(
        _rmsnorm_body,
        grid=(_cdiv(M, tr),),
        in_specs=[pl.BlockSpec((tr, D), lambda i: (i, 0)), pl.BlockSpec((1, D), lambda i: (0, 0))],
        out_specs=pl.BlockSpec((tr, D), lambda i: (i, 0)),
        out_shape=jax.ShapeDtypeStruct((M, D), BF16),
        compiler_params=_params(("parallel",)),
        name="rmsnorm_bf16",
    )(x, w)


def _router_body(x_ref, w_ref, r_ref, idx_ref, gate_ref):
    x = x_ref[...]
    ms = jnp.mean(x * x, axis=-1, keepdims=True)
    h = x * lax.rsqrt(ms + RMS_EPS) * w_ref[...]
    logits = jnp.dot(h, r_ref[...], preferred_element_type=F32, precision=lax.Precision.HIGHEST)
    lane = lax.broadcasted_iota(I32, logits.shape, 1)
    neg = jnp.float32(-jnp.inf)
    l1 = jnp.where(lane < N_EXPERTS, logits, neg)
    m1 = jnp.max(l1, axis=-1, keepdims=True)
    i1 = jnp.min(jnp.where(l1 == m1, lane, LANES), axis=-1, keepdims=True)
    l2 = jnp.where(lane == i1, neg, l1)
    m2 = jnp.max(l2, axis=-1, keepdims=True)
    i2 = jnp.min(jnp.where(l2 == m2, lane, LANES), axis=-1, keepdims=True)
    e2 = jnp.exp(m2 - m1)
    den = 1.0 + e2
    idx_ref[...] = jnp.where(lane == 0, i1, jnp.where(lane == 1, i2, 0))
    gate_ref[...] = jnp.where(lane == 0, 1.0 / den, jnp.where(lane == 1, e2 / den, 0.0))


def route_top2(x, w, router_pad, tr=512):
    M, D = x.shape
    idx, gate = pl.pallas_call(
        _router_body,
        grid=(_cdiv(M, tr),),
        in_specs=[pl.BlockSpec((tr, D), lambda i: (i, 0)), pl.BlockSpec((1, D), lambda i: (0, 0)),
                  pl.BlockSpec((D, LANES), lambda i: (0, 0))],
        out_specs=[pl.BlockSpec((tr, LANES), lambda i: (i, 0)), pl.BlockSpec((tr, LANES), lambda i: (i, 0))],
        out_shape=[jax.ShapeDtypeStruct((M, LANES), I32), jax.ShapeDtypeStruct((M, LANES), F32)],
        compiler_params=_params(("parallel",)),
        name="route_top2",
    )(x, w, router_pad)
    return idx[:, :TOP_K], gate[:, :TOP_K]


def _first_of_weight(te_ref, m):
    prev = te_ref[jnp.maximum(m - 1, 0)]
    return jnp.logical_or(m == 0, te_ref[m] != prev)


def _for_rows(mode_ref, m, parts, fn, o_ref):
    mode = mode_ref[m]
    pl.when(mode == MODE_FULL)(lambda: fn(slice(None)))

    def zero(sl):
        o_ref[sl, :] = jnp.zeros(o_ref[sl, :].shape, o_ref.dtype)

    pl.when(mode == MODE_SKIP)(lambda: zero(slice(None)))
    if parts:
        tm = o_ref.shape[0]
        assert tm % PART_ROWS == 0
        for c in range(tm // PART_ROWS):
            sl = slice(c * PART_ROWS, (c + 1) * PART_ROWS)
            pl.when(jnp.logical_and(mode >= MODE_PART, c < mode - MODE_PART))(lambda sl=sl: fn(sl))
            pl.when(jnp.logical_and(mode >= MODE_PART, c >= mode - MODE_PART))(lambda sl=sl: zero(sl))


def _proj_body(te_ref, mode_ref, a_ref, w_ref, o_ref, wb_ref, *, parts):
    m = pl.program_id(1)
    pl.when(_first_of_weight(te_ref, m))(lambda: wb_ref.__setitem__(Ellipsis, w_ref[...].astype(BF16)))

    def rows(sl):
        o_ref[sl, :] = jnp.dot(a_ref[sl, :], wb_ref[...], preferred_element_type=F32)

    _for_rows(mode_ref, m, parts, rows, o_ref)


def matmul_proj(a, w, te, mode, *, n_lo, n_cols, tm, tn, parts=False):
    M, K = a.shape
    nb0 = n_lo // tn
    assert nb0 * tn == n_lo
    grid = (_cdiv(n_cols, tn), _cdiv(M, tm))
    return pl.pallas_call(
        functools.partial(_proj_body, parts=parts),
        grid_spec=pltpu.PrefetchScalarGridSpec(
            num_scalar_prefetch=2, grid=grid,
            in_specs=[pl.BlockSpec((tm, K), lambda n, m, te, md: (m, 0)),
                      pl.BlockSpec((None, K, tn), lambda n, m, te, md: (te[m], 0, nb0 + n))],
            out_specs=pl.BlockSpec((tm, tn), lambda n, m, te, md: (m, n)),
            scratch_shapes=[pltpu.VMEM((K, tn), BF16)]),
        out_shape=jax.ShapeDtypeStruct((M, n_cols), F32),
        compiler_params=_params(("arbitrary", "arbitrary")),
        name="matmul_proj",
    )(te, mode, a, w)


def _next_weight(te):
    big = jnp.int32(2 ** 30)
    nxt = jnp.min(jnp.where(te[None, :] > te[:, None], te[None, :], big), axis=1)
    return jnp.where(nxt == big, -1, nxt).astype(I32)


def _stream_weights(te_ref, nxt_ref, w_hbms, wf_refs, wb_refs, sems, *, k_lo, tk, tn):
    n, m = pl.program_id(0), pl.program_id(1)

    def copy(i, e, nn):
        cols = pl.ds(pl.multiple_of(nn * tn, tn), tn)
        return pltpu.make_async_copy(w_hbms[i].at[e, pl.ds(k_lo, tk), cols], wf_refs[i], sems.at[i])

    @pl.when(_first_of_weight(te_ref, m))
    def _():
        e = te_ref[m]

        @pl.when(jnp.logical_and(n == 0, m == 0))
        def _():
            for i in range(len(w_hbms)):
                copy(i, e, n).start()

        for i in range(len(w_hbms)):
            copy(i, e, n).wait()
            wb_refs[i][...] = wf_refs[i][...].astype(BF16)
        nxt = nxt_ref[m]
        same_sweep = nxt >= 0
        e2 = jnp.where(same_sweep, nxt, te_ref[0])
        n2 = jnp.where(same_sweep, n, n + 1)

        @pl.when(jnp.logical_or(same_sweep, n + 1 < pl.num_programs(0)))
        def _():
            for i in range(len(w_hbms)):
                copy(i, e2, n2).start()


def _ffn1_body(te_ref, mode_ref, nxt_ref, a_ref, w1_hbm, w3_hbm, o_ref, wf1_ref, wf3_ref, wb1_ref, wb3_ref, sems,
               *, parts, tn):
    m = pl.program_id(1)
    _stream_weights(te_ref, nxt_ref, (w1_hbm, w3_hbm), (wf1_ref, wf3_ref), (wb1_ref, wb3_ref), sems,
                    k_lo=0, tk=wf1_ref.shape[0], tn=tn)

    def rows(sl):
        a = a_ref[sl, :]
        u1 = jnp.dot(a, wb1_ref[...], preferred_element_type=F32)
        u3 = jnp.dot(a, wb3_ref[...], preferred_element_type=F32)
        o_ref[sl, :] = (u1 * jax.nn.sigmoid(u1) * u3).astype(o_ref.dtype)

    _for_rows(mode_ref, m, parts, rows, o_ref)


def matmul_swiglu_in(a, w1, w3, te, mode, *, tm, tn, parts=False):
    M, K = a.shape
    F = w1.shape[-1]
    assert F % tn == 0
    grid = (F // tn, _cdiv(M, tm))
    wspec = pl.BlockSpec(memory_space=pl.ANY)
    return pl.pallas_call(
        functools.partial(_ffn1_body, parts=parts, tn=tn),
        grid_spec=pltpu.PrefetchScalarGridSpec(
            num_scalar_prefetch=3, grid=grid,
            in_specs=[pl.BlockSpec((tm, K), lambda n, m, te, md, nx: (m, 0)), wspec, wspec],
            out_specs=pl.BlockSpec((tm, tn), lambda n, m, te, md, nx: (m, n)),
            scratch_shapes=[pltpu.VMEM((K, tn), F32), pltpu.VMEM((K, tn), F32),
                            pltpu.VMEM((K, tn), BF16), pltpu.VMEM((K, tn), BF16),
                            pltpu.SemaphoreType.DMA((2,))]),
        out_shape=jax.ShapeDtypeStruct((M, F), BF16),
        compiler_params=_params(("arbitrary", "arbitrary")),
        name="matmul_swiglu_in",
    )(te, mode, _next_weight(te), a, w1, w3)


def _res_body(te_ref, mode_ref, nxt_ref, a_ref, w_hbm, res_ref, *rest, parts, scaled, k_lo, tn):
    if scaled:
        scale_ref, o_ref, wf_ref, wb_ref, sems = rest
    else:
        o_ref, wf_ref, wb_ref, sems = rest
    m = pl.program_id(1)
    _stream_weights(te_ref, nxt_ref, (w_hbm,), (wf_ref,), (wb_ref,), sems, k_lo=k_lo, tk=wf_ref.shape[0], tn=tn)

    def rows(sl):
        d = jnp.dot(a_ref[sl, :], wb_ref[...], preferred_element_type=F32)
        if scaled:
            d = d * scale_ref[sl, :]
        o_ref[sl, :] = res_ref[sl, :] + d

    _for_rows(mode_ref, m, parts, rows, o_ref)


def matmul_residual(a, w, res, te, mode, *, tm, tn, tk, parts=False, scale=None):
    M, K = a.shape
    N = w.shape[-1]
    assert K % tk == 0 and N % tn == 0
    grid = (N // tn, _cdiv(M, tm))
    nxt = _next_weight(te)
    for kb in range(K // tk):
        in_specs = [pl.BlockSpec((tm, tk), lambda n, m, te, md, nx, kb=kb: (m, kb)),
                    pl.BlockSpec(memory_space=pl.ANY),
                    pl.BlockSpec((tm, tn), lambda n, m, te, md, nx: (m, n))]
        args = [te, mode, nxt, a, w, res]
        if scale is not None:
            in_specs.append(pl.BlockSpec((tm, 1), lambda n, m, te, md, nx: (m, 0)))
            args.append(scale)
        res = pl.pallas_call(
            functools.partial(_res_body, parts=parts, scaled=scale is not None, k_lo=kb * tk, tn=tn),
            grid_spec=pltpu.PrefetchScalarGridSpec(
                num_scalar_prefetch=3, grid=grid, in_specs=in_specs,
                out_specs=pl.BlockSpec((tm, tn), lambda n, m, te, md, nx: (m, n)),
                scratch_shapes=[pltpu.VMEM((tk, tn), F32), pltpu.VMEM((tk, tn), BF16),
                                pltpu.SemaphoreType.DMA((1,))]),
            out_shape=jax.ShapeDtypeStruct((M, N), F32),
            input_output_aliases={5: 0},
            compiler_params=_params(("arbitrary", "arbitrary")),
            name="matmul_residual",
        )(*args)
    return res


def _dense_tiles(M, tm, e):
    assert M % tm == 0
    return jnp.full((M // tm,), e, I32), jnp.full((M // tm,), MODE_FULL, I32)


_hp_dot = functools.partial(jnp.dot, preferred_element_type=F32, precision=lax.Precision.HIGHEST)


def _small_body(*refs, kind, normed):
    it = iter(refs)
    x_ref = next(it)
    nw_ref = next(it) if normed else None
    a = x_ref[...]
    if normed:
        a = a * lax.rsqrt(jnp.mean(a * a, axis=-1, keepdims=True) + RMS_EPS) * nw_ref[...]
    if kind == 'swiglu':
        w1_ref, w3_ref, o_ref = it
        u1 = _hp_dot(a, w1_ref[...])
        u3 = _hp_dot(a, w3_ref[...])
        o_ref[...] = u1 * jax.nn.sigmoid(u1) * u3
    elif kind == 'res':
        w_ref, res_ref, o_ref = it
        o_ref[...] = res_ref[...] + _hp_dot(a, w_ref[...])
    else:
        w_ref, o_ref = it
        o_ref[...] = _hp_dot(a, w_ref[...])


def small_matmul(kind, x, ws, e, *, tn, norm_w=None, res=None, n_lo=0, n_cols=None):
    R, K = x.shape
    N = ws[0].shape[-1]
    n_cols = N - n_lo if n_cols is None else n_cols
    nb0 = n_lo // tn
    assert nb0 * tn == n_lo
    normed = norm_w is not None
    row = lambda w: pl.BlockSpec((R, w), lambda n: (0, 0))
    in_specs, args = [row(K)], [x]
    if normed:
        in_specs.append(pl.BlockSpec((1, K), lambda n: (0, 0)))
        args.append(norm_w)
    for w in ws:
        in_specs.append(pl.BlockSpec((None, K, tn), lambda n: (e, 0, nb0 + n)))
        args.append(w)
    ospec = pl.BlockSpec((R, tn), lambda n: (0, n))
    aliases = {}
    if kind == 'res':
        in_specs.append(ospec)
        aliases = {len(args): 0}
        args.append(res)
    return pl.pallas_call(
        functools.partial(_small_body, kind=kind, normed=normed),
        grid=(_cdiv(n_cols, tn),), in_specs=in_specs, out_specs=ospec,
        out_shape=jax.ShapeDtypeStruct((R, n_cols), F32),
        input_output_aliases=aliases,
        compiler_params=_params(("arbitrary",)),
        name="small_matmul_" + kind,
    )(*args)


def _gather_norm_body(src_ref, xp_hbm, xs_hbm, w_ref, o_ref, buf, sem, *, tg, mp):
    def issue(r, c):
        t = src_ref[r]
        dst = buf.at[pl.ds(r, 1), :]
        pl.when(t < mp)(lambda: pltpu.make_async_copy(xp_hbm.at[pl.ds(t, 1), :], dst, sem).start())
        pl.when(t >= mp)(lambda: pltpu.make_async_copy(xs_hbm.at[pl.ds(t - mp, 1), :], dst, sem).start())
        return c

    lax.fori_loop(0, tg, issue, 0, unroll=DMA_UNROLL)

    def wait(r, c):
        pltpu.make_async_copy(xp_hbm.at[pl.ds(0, 1), :], buf.at[pl.ds(r, 1), :], sem).wait()
        return c

    lax.fori_loop(0, tg, wait, 0, unroll=DMA_UNROLL)
    x = buf[...]
    ms = jnp.mean(x * x, axis=-1, keepdims=True)
    o_ref[...] = (x * lax.rsqrt(ms + RMS_EPS) * w_ref[...]).astype(o_ref.dtype)


def gather_rmsnorm_bf16(xp, xs, w, src, tg=256):
    Mp, D = xp.shape
    R = src.shape[0]
    assert R % tg == 0
    return pl.pallas_call(
        functools.partial(_gather_norm_body, tg=tg, mp=Mp),
        grid=(R // tg,),
        in_specs=[pl.BlockSpec((tg,), lambda i: (i,), memory_space=pltpu.SMEM),
                  pl.BlockSpec(memory_space=pl.ANY), pl.BlockSpec(memory_space=pl.ANY),
                  pl.BlockSpec((1, D), lambda i: (0, 0))],
        out_specs=pl.BlockSpec((tg, D), lambda i: (i, 0)),
        out_shape=jax.ShapeDtypeStruct((R, D), BF16),
        scratch_shapes=[pltpu.VMEM((tg, D), F32), pltpu.SemaphoreType.DMA(())],
        compiler_params=_params(("arbitrary",)),
        name="gather_rmsnorm_bf16",
    )(src, xp, xs, w)


def _combine_body(s0_ref, s1_ref, x_ref, y_hbm, o_ref, buf0, buf1, sem, *, tc):
    def issue(r, c):
        pltpu.make_async_copy(y_hbm.at[pl.ds(s0_ref[r], 1), :], buf0.at[pl.ds(r, 1), :], sem).start()
        pltpu.make_async_copy(y_hbm.at[pl.ds(s1_ref[r], 1), :], buf1.at[pl.ds(r, 1), :], sem).start()
        return c

    lax.fori_loop(0, tc, issue, 0, unroll=DMA_UNROLL)

    def wait(r, c):
        pltpu.make_async_copy(y_hbm.at[pl.ds(0, 1), :], buf0.at[pl.ds(r, 1), :], sem).wait()
        pltpu.make_async_copy(y_hbm.at[pl.ds(0, 1), :], buf1.at[pl.ds(r, 1), :], sem).wait()
        return c

    lax.fori_loop(0, tc, wait, 0, unroll=DMA_UNROLL)
    o_ref[...] = x_ref[...] + (buf0[...] + buf1[...])


def combine_top2(x, ys, slot0, slot1, tc=256):
    M, D = x.shape
    nt = _cdiv(M, tc)
    assert slot0.shape[0] == nt * tc
    sspec = pl.BlockSpec((tc,), lambda i: (i,), memory_space=pltpu.SMEM)
    return pl.pallas_call(
        functools.partial(_combine_body, tc=tc),
        grid=(nt,),
        in_specs=[sspec, sspec, pl.BlockSpec((tc, D), lambda i: (i, 0)), pl.BlockSpec(memory_space=pl.ANY)],
        out_specs=pl.BlockSpec((tc, D), lambda i: (i, 0)),
        out_shape=jax.ShapeDtypeStruct((M, D), F32),
        scratch_shapes=[pltpu.VMEM((tc, D), F32), pltpu.VMEM((tc, D), F32), pltpu.SemaphoreType.DMA(())],
        compiler_params=_params(("arbitrary",)),
        name="combine_top2",
    )(slot0, slot1, x, ys)


def _route_tables(idx, gates, tmg, n_tiles):
    M = idx.shape[0]
    e = idx.reshape(-1)
    onehot = (e[:, None] == jnp.arange(N_EXPERTS, dtype=I32)[None, :]).astype(I32)
    csum = jnp.cumsum(onehot, axis=0)
    rank = jnp.take_along_axis(csum, e[:, None], axis=1)[:, 0] - 1
    counts = csum[-1]
    tiles_e = (counts + tmg - 1) // tmg
    tile_end = jnp.cumsum(tiles_e)
    tile_start = tile_end - tiles_e
    slot = tile_start[e] * tmg + rank
    tiles = jnp.arange(n_tiles, dtype=I32)
    te_raw = jnp.sum((tiles[:, None] >= tile_end[None, :]).astype(I32), axis=1)
    total = tile_end[-1]
    valid = tiles < total
    te_last = jnp.sum((total - 1 >= tile_end).astype(I32))
    te = jnp.where(valid, te_raw, te_last).astype(I32)
    tile_rows = jnp.clip(counts[te] - (tiles - tile_start[te]) * tmg, 0, tmg)
    parts = (tile_rows + PART_ROWS - 1) // PART_ROWS
    mode = jnp.where(valid, jnp.where(parts * PART_ROWS >= tmg, MODE_FULL, MODE_PART + parts), MODE_SKIP)
    mode = mode.astype(I32)
    rows = n_tiles * tmg
    src = jnp.zeros((rows,), I32).at[slot].set(jnp.arange(2 * M, dtype=I32) // 2)
    gate_s = jnp.zeros((rows,), F32).at[slot].set(gates.reshape(-1))
    return te, mode, src, gate_s[:, None], slot.reshape(M, TOP_K)


def _rope_tables(pos):
    half = ROPE_DIM // 2
    inv_freq = jnp.exp(-math.log(ROPE_THETA) * 2.0 * jnp.arange(half, dtype=F32) / ROPE_DIM)
    ang = pos.astype(F32)[:, None] * inv_freq[None, :]
    cos, sin = jnp.cos(ang), jnp.sin(ang)
    T = pos.shape[0]
    rest = HEAD_DIM - ROPE_DIM
    ctab = jnp.concatenate([cos, cos, jnp.ones((T, rest), F32)], axis=1)
    stab = jnp.concatenate([-sin, sin, jnp.zeros((T, rest), F32)], axis=1)
    d = jnp.arange(HEAD_DIM)
    partner = jnp.where(d < half, d + half, jnp.where(d < ROPE_DIM, d - half, d))
    swap = (d[:, None] == partner[None, :]).astype(BF16)
    return ctab, stab, swap


def _head_norm_rope(x, nw, ctab, stab, swap):
    y = x * lax.rsqrt(jnp.mean(x * x, axis=-1, keepdims=True) + RMS_EPS) * nw
    return y * ctab + _dot_exact_rhs(y, swap) * stab


def _norm_rope_wide(x, nw, ctab, stab, gmat):
    outs = []
    lane = lax.broadcasted_iota(I32, (x.shape[0], GROUP), 1) % HEAD_DIM
    half = ROPE_DIM // 2
    for q in range(x.shape[1] // GROUP):
        xg = x[:, q * GROUP:(q + 1) * GROUP]
        sq = xg * xg
        hi = sq.astype(BF16)
        lo = (sq - hi.astype(F32)).astype(BF16)
        ss = jnp.dot(hi, gmat, preferred_element_type=F32) + jnp.dot(lo, gmat, preferred_element_type=F32)
        y = xg * lax.rsqrt(ss * (1.0 / HEAD_DIM) + RMS_EPS) * nw
        partner = jnp.where(lane < half, pltpu.roll(y, GROUP - half, axis=1), pltpu.roll(y, half, axis=1))
        outs.append(y * ctab + partner * stab)
    return outs


def _attn_prompt_body(q_ref, kc_ref, kp_ref, vc_ref, vp_ref, cc_ref, sc_ref, cp_ref, sp_ref, qn_ref, kn_ref,
                      sink_ref, g_ref, o_ref, knew_ref):
    i = pl.program_id(1)
    gmat = g_ref[...]
    cc, sc, cp, sp = cc_ref[...], sc_ref[...], cp_ref[...], sp_ref[...]
    qn, kn = qn_ref[...], kn_ref[...]
    R = Q_PER_KV * BLOCK
    row = lax.broadcasted_iota(I32, (R, 2 * BLOCK), 0)
    col = lax.broadcasted_iota(I32, (R, 2 * BLOCK), 1)
    rel = (row % BLOCK) - col + BLOCK
    mask = (rel >= 0) & (rel <= WINDOW) & jnp.logical_not((i == 0) & (col < BLOCK))
    rowh = lax.broadcasted_iota(I32, (R, 1), 0) // BLOCK
    headmask = (lax.broadcasted_iota(I32, (R, GROUP), 0) // BLOCK
                == lax.broadcasted_iota(I32, (R, GROUP), 1) // HEAD_DIM)
    lane_lo = lax.broadcasted_iota(I32, (2 * BLOCK, LANES), 1) < HEAD_DIM
    heads_per_group = GROUP // HEAD_DIM

    q_rot = _norm_rope_wide(q_ref[...], qn, cc, sc, gmat)
    k_cur = _norm_rope_wide(kc_ref[...], kn, cc, sc, gmat)
    k_prev = _norm_rope_wide(kp_ref[...], kn, cp, sp, gmat)
    for j in range(KV_WIDTH // GROUP):
        knew_ref[:, j * GROUP:(j + 1) * GROUP] = k_cur[j]

    def spread(x, g):
        other = pltpu.roll(x, HEAD_DIM, axis=1)
        one = jnp.where(lane_lo, x, other) if g % 2 == 0 else jnp.where(lane_lo, other, x)
        return jnp.concatenate([one, one], axis=1)

    for g in range(N_KV_HEADS):
        j, m = g // heads_per_group, (g % heads_per_group) // 2
        lanes = slice(m * LANES, (m + 1) * LANES)
        kcat = jnp.concatenate([k_prev[j][:, lanes], k_cur[j][:, lanes]], axis=0)
        c0 = g // 2 * LANES
        vcat = jnp.concatenate([vp_ref[:, c0:c0 + LANES], vc_ref[:, c0:c0 + LANES]], axis=0)
        kx = spread(kcat, g).astype(BF16)
        vx = spread(vcat, g).astype(BF16)
        qbd = jnp.where(headmask, jnp.concatenate([q_rot[g]] * Q_PER_KV, axis=0), 0.0).astype(BF16)
        s = lax.dot_general(qbd, kx, (((1,), (1,)), ((), ())), preferred_element_type=F32) * ATT_SCALE
        s = jnp.where(mask, s, -jnp.inf)
        sink = jnp.zeros((R, 1), F32)
        for h in range(Q_PER_KV):
            sink = jnp.where(rowh == h, sink_ref[g * Q_PER_KV + h], sink)
        mx = jnp.maximum(jnp.max(s, axis=-1, keepdims=True), sink)
        p = jnp.exp(s - mx)
        denom = jnp.sum(p, axis=-1, keepdims=True) + jnp.exp(sink - mx)
        o = jnp.dot((p / denom).astype(BF16), vx, preferred_element_type=F32)
        o = jnp.where(headmask, o, 0.0)
        out = o[0:BLOCK]
        for h in range(1, Q_PER_KV):
            out = out + o[h * BLOCK:(h + 1) * BLOCK]
        o_ref[:, g * GROUP:(g + 1) * GROUP] = out.astype(o_ref.dtype)


def attention_prompt(u_att, B, T, q_norm, k_norm, sinks):
    nb = T // BLOCK
    reps = GROUP // HEAD_DIM
    ctab, stab, _ = _rope_tables(jnp.arange(T, dtype=I32))
    ctab, stab = jnp.tile(ctab, (1, reps)), jnp.tile(stab, (1, reps))
    kcol, vcol = ATT_WIDTH // KV_WIDTH, ATT_WIDTH // KV_WIDTH + 1
    cur = lambda b, i: b * nb + i
    prev = lambda b, i: b * nb + jnp.maximum(i - 1, 0)
    tspec_c = pl.BlockSpec((BLOCK, GROUP), lambda b, i: (i, 0))
    tspec_p = pl.BlockSpec((BLOCK, GROUP), lambda b, i: (jnp.maximum(i - 1, 0), 0))
    wspec = pl.BlockSpec((1, GROUP), lambda b, i: (0, 0))
    return pl.pallas_call(
        _attn_prompt_body,
        grid=(B, nb),
        in_specs=[pl.BlockSpec((BLOCK, ATT_WIDTH), lambda b, i: (cur(b, i), 0)),
                  pl.BlockSpec((BLOCK, KV_WIDTH), lambda b, i: (cur(b, i), kcol)),
                  pl.BlockSpec((BLOCK, KV_WIDTH), lambda b, i: (prev(b, i), kcol)),
                  pl.BlockSpec((BLOCK, KV_WIDTH), lambda b, i: (cur(b, i), vcol)),
                  pl.BlockSpec((BLOCK, KV_WIDTH), lambda b, i: (prev(b, i), vcol)),
                  tspec_c, tspec_c, tspec_p, tspec_p, wspec, wspec,
                  pl.BlockSpec(memory_space=pltpu.SMEM),
                  pl.BlockSpec((GROUP, GROUP), lambda b, i: (0, 0))],
        out_specs=[pl.BlockSpec((BLOCK, ATT_WIDTH), lambda b, i: (cur(b, i), 0)),
                   pl.BlockSpec((BLOCK, KV_WIDTH), lambda b, i: (b, 0))],
        out_shape=[jax.ShapeDtypeStruct((B * T, ATT_WIDTH), BF16),
                   jax.ShapeDtypeStruct((B * BLOCK, KV_WIDTH), F32)],
        compiler_params=_params(("arbitrary", "arbitrary")),
        name="attention_prompt",
    )(u_att, u_att, u_att, u_att, u_att, ctab, stab, ctab, stab, jnp.tile(q_norm, reps)[None, :],
      jnp.tile(k_norm, reps)[None, :], sinks, _head_sum_matrix())


def _attn_sample_body(u_ref, k_ref, v_ref, ct_ref, st_ref, qn_ref, kn_ref, sink_ref, swap_ref, rexp_ref, bmask_ref,
                      o_ref, newk_ref, newv_ref, q_s, kx_s, vx_s, kn_s):
    b = pl.program_id(0)
    rexp = rexp_ref[...]

    @pl.when(b == 0)
    def _():
        swap = swap_ref[...]
        ct, st = ct_ref[...], st_ref[...]
        for h in range(N_Q_HEADS):
            sl = slice(h * HEAD_DIM, (h + 1) * HEAD_DIM)
            q_s[:, sl] = _head_norm_rope(u_ref[:, sl], qn_ref[...], ct, st, swap)
        for g in range(N_KV_HEADS):
            sl = slice(g * HEAD_DIM, (g + 1) * HEAD_DIM)
            kn_s[:, sl] = _head_norm_rope(u_ref[:, ATT_WIDTH + g * HEAD_DIM:ATT_WIDTH + (g + 1) * HEAD_DIM],
                                          kn_ref[...], ct, st, swap)
        kx_s[...] = _dot_exact_rhs(kn_s[...], rexp)
        vx_s[...] = _dot_exact_rhs(u_ref[:, ATT_WIDTH + KV_WIDTH:], rexp)

    bmask = bmask_ref[...]
    qfull = q_s[pl.ds(b, 1), :] * bmask
    kb, vb = k_ref[...], v_ref[...]
    kexp = _dot_exact_rhs(kb, rexp)
    vexp = _dot_exact_rhs(vb, rexp)
    s = lax.dot_general(qfull, kexp, (((1,), (1,)), ((), ())), preferred_element_type=F32,
                        precision=lax.Precision.HIGHEST) * ATT_SCALE
    s_self = jnp.sum(qfull * kx_s[pl.ds(b, 1), :], axis=-1, keepdims=True) * ATT_SCALE
    sink = sink_ref[...]
    m = jnp.maximum(jnp.maximum(jnp.max(s, axis=-1, keepdims=True), s_self), sink)
    p = jnp.exp(s - m)
    p_self = jnp.exp(s_self - m)
    denom = jnp.sum(p, axis=-1, keepdims=True) + p_self + jnp.exp(sink - m)
    o = _hp_dot(p / denom, vexp)
    o = o + (p_self / denom) * vx_s[pl.ds(b, 1), :]
    o_ref[pl.ds(b, 1), :] = jnp.sum(o * bmask, axis=0, keepdims=True)
    last = lax.broadcasted_iota(I32, kb.shape, 0) == WINDOW - 1
    newk_ref[...] = jnp.where(last, kn_s[pl.ds(b, 1), :], pltpu.roll(kb, WINDOW - 1, axis=0))
    newv_ref[...] = jnp.where(last, u_ref[pl.ds(b, 1), ATT_WIDTH + KV_WIDTH:], pltpu.roll(vb, WINDOW - 1, axis=0))


def attention_sample(u_att, row0, Bs, cache_k, cache_v, layer, q_norm, k_norm, sinks):
    assert row0 % Bs == 0
    ctab, stab, swap = _rope_tables(jnp.full((1,), PAST_LEN, I32))
    gd = jnp.arange(KV_WIDTH)
    hd = jnp.arange(ATT_WIDTH)
    rexp = ((gd[:, None] // HEAD_DIM == hd[None, :] // (HEAD_DIM * Q_PER_KV))
            & (gd[:, None] % HEAD_DIM == hd[None, :] % HEAD_DIM)).astype(BF16)
    bmask = (jnp.arange(N_Q_HEADS)[:, None] == hd[None, :] // HEAD_DIM).astype(F32)
    full = lambda shape: pl.BlockSpec(shape, lambda b: (0,) * len(shape))
    cspec = pl.BlockSpec((None, None, WINDOW, KV_WIDTH), lambda b: (layer, b, 0, 0))
    ospec = pl.BlockSpec((None, WINDOW, KV_WIDTH), lambda b: (b, 0, 0))
    return pl.pallas_call(
        _attn_sample_body,
        grid=(Bs,),
        in_specs=[pl.BlockSpec((Bs, ATT_PROJ), lambda b: (row0 // Bs, 0)), cspec, cspec,
                  full((1, HEAD_DIM)), full((1, HEAD_DIM)), full((1, HEAD_DIM)), full((1, HEAD_DIM)),
                  full((N_Q_HEADS, 1)), full((HEAD_DIM, HEAD_DIM)), full((KV_WIDTH, ATT_WIDTH)),
                  full((N_Q_HEADS, ATT_WIDTH))],
        out_specs=[full((Bs, ATT_WIDTH)), ospec, ospec],
        out_shape=[jax.ShapeDtypeStruct((Bs, ATT_WIDTH), F32),
                   jax.ShapeDtypeStruct((Bs, WINDOW, KV_WIDTH), F32),
                   jax.ShapeDtypeStruct((Bs, WINDOW, KV_WIDTH), F32)],
        scratch_shapes=[pltpu.VMEM((Bs, ATT_WIDTH), F32), pltpu.VMEM((Bs, ATT_WIDTH), F32),
                        pltpu.VMEM((Bs, ATT_WIDTH), F32), pltpu.VMEM((Bs, KV_WIDTH), F32)],
        compiler_params=_params(("arbitrary",)),
        name="attention_sample",
    )(u_att, cache_k, cache_v, ctab, stab, q_norm[None, :], k_norm[None, :], sinks[:, None], swap, rexp, bmask)


def _head_sum_matrix():
    i = jnp.arange(GROUP)
    return (i[:, None] // RWKV_HEAD == i[None, :] // RWKV_HEAD).astype(BF16)


def _head_sum(x, gmat):
    return jnp.concatenate([_dot_exact_rhs(x[:, q * GROUP:(q + 1) * GROUP], gmat) for q in range(N_GROUPS)], axis=1)


def _rwkv_prep_body(*refs, shifted, has_vres):
    it = iter(refs)
    u_ref = next(it)
    prev_ref = next(it)
    mu_ref, w0_ref, a0_ref, kk_ref, ka_ref, rk_ref, lora_ref, g_ref = (next(it) for _ in range(8))
    if has_vres:
        v0_ref, v1_ref, v2_ref, vf_ref = (next(it) for _ in range(4))
    r_o, w_o, k_o, v_o, a_o, b_o, g_o, bonus_o, tail_s = it
    u = u_ref[...]
    if shifted:
        first = jnp.where(pl.program_id(1) > 0, prev_ref[SUBLANES - 1:SUBLANES, :], 0.0)
        prev = jnp.where(lax.broadcasted_iota(I32, u.shape, 0) == 0, first, pltpu.roll(u, 1, axis=0))
    else:
        prev = prev_ref[...]
    xs = u + (prev - u) * mu_ref[...]
    r = xs[:, :RWKV_WIDTH]
    k = xs[:, RWKV_WIDTH:2 * RWKV_WIDTH]
    v = xs[:, 2 * RWKV_WIDTH:3 * RWKV_WIDTH]
    tail_s[...] = jnp.zeros(tail_s.shape, F32)
    tail_s[:, :LORA_IN] = xs[:, 3 * RWKV_WIDTH:]
    t = tail_s[...]
    def dot(x, w):
        if w.dtype == F32:
            return _hp_dot(x, w)
        return jnp.dot(x.astype(BF16), w, preferred_element_type=F32)

    dw = dot(jnp.tanh(t), lora_ref[0])
    da = dot(t, lora_ref[1])
    g = dot(jax.nn.sigmoid(t), lora_ref[2])
    z = -(w0_ref[...] + dw)
    softplus = jnp.maximum(z, 0.0) + jnp.log(1.0 + jnp.exp(-jnp.abs(z)))
    w_log = -softplus - 0.5
    decay = jnp.exp(-jnp.exp(w_log))
    a = jax.nn.sigmoid(a0_ref[...] + da)
    if has_vres:
        lo = dot(dot(v, v1_ref[...]), v2_ref[...])
        v = v + (vf_ref[...] - v) * jax.nn.sigmoid(v0_ref[...] + lo)
    gmat = g_ref[...]
    kk = k * kk_ref[...]
    kk = kk / jnp.maximum(jnp.sqrt(_head_sum(kk * kk, gmat)), 1e-12)
    k = k * (1.0 + (a - 1.0) * ka_ref[...])
    r_o[...] = r
    w_o[...] = decay
    k_o[...] = k
    v_o[...] = v
    a_o[...] = -kk
    b_o[...] = kk * a
    g_o[...] = g
    bonus_o[...] = _head_sum(r * k * rk_ref[...], gmat) * v


def rwkv_prep(u_rw, row0, B, T, prev_rows, p, v_first, tr):
    n = B * T
    shifted = prev_rows is None
    if shifted:
        assert T % tr == 0 and row0 == 0
        nt = T // tr
        grid = (B, nt)
        rowblk = lambda b, i: b * nt + i
        prev_spec = pl.BlockSpec((SUBLANES, SHIFT_WIDTH),
                                 lambda b, i: (jnp.maximum(rowblk(b, i) * (tr // SUBLANES) - 1, 0), 0))
        prev_arr = u_rw
        oblk = lambda b, i: (b * nt + i, 0)
    else:
        assert T == 1 and tr == B and row0 % tr == 0
        grid = (1, 1)
        rowblk = lambda b, i: row0 // tr
        prev_spec = pl.BlockSpec((tr, SHIFT_WIDTH), lambda b, i: (0, 0))
        prev_arr = prev_rows
        oblk = lambda b, i: (0, 0)
    vec = lambda w: pl.BlockSpec((1, w), lambda b, i: (0, 0))
    in_specs = [pl.BlockSpec((tr, SHIFT_WIDTH), lambda b, i: (rowblk(b, i), 0)), prev_spec,
                vec(SHIFT_WIDTH), vec(RWKV_WIDTH), vec(RWKV_WIDTH), vec(RWKV_WIDTH), vec(RWKV_WIDTH),
                vec(RWKV_WIDTH), pl.BlockSpec((3, LORA_PAD, RWKV_WIDTH), lambda b, i: (0, 0, 0)),
                pl.BlockSpec((GROUP, GROUP), lambda b, i: (0, 0))]
    args = [u_rw, prev_arr, p['mu'], p['w0'], p['a0'], p['k_k'], p['k_a'], p['r_k'], p['lora'], _head_sum_matrix()]
    has_vres = v_first is not None
    if has_vres:
        in_specs += [vec(RWKV_WIDTH), pl.BlockSpec((RWKV_WIDTH, MV_LORA), lambda b, i: (0, 0)),
                     pl.BlockSpec((MV_LORA, RWKV_WIDTH), lambda b, i: (0, 0)),
                     pl.BlockSpec((tr, RWKV_WIDTH), oblk)]
        args += [p['v0'], p['v1'], p['v2'], v_first]
    ospec = pl.BlockSpec((tr, RWKV_WIDTH), oblk)
    return pl.pallas_call(
        functools.partial(_rwkv_prep_body, shifted=shifted, has_vres=has_vres),
        grid=grid, in_specs=in_specs, out_specs=[ospec] * 8,
        out_shape=[jax.ShapeDtypeStruct((n, RWKV_WIDTH), F32)] * 8,
        scratch_shapes=[pltpu.VMEM((tr, LORA_PAD), F32)],
        compiler_params=_params(("arbitrary", "arbitrary")),
        name="rwkv_prep",
    )(*args)


def _rwkv_params(l, shift_mu, w0, w2, a0, a2, g2, k_k, k_a, r_k, v0, v1, v2):
    lora = jnp.zeros((3, LORA_PAD, RWKV_WIDTH), F32)
    lora = lora.at[0, :DECAY_LORA].set(w2[l])
    lora = lora.at[1, DECAY_LORA:DECAY_LORA + AAA_LORA].set(a2[l])
    lora = lora.at[2, DECAY_LORA + AAA_LORA:LORA_IN].set(g2[l])
    p = {'mu': shift_mu[l][None, :], 'w0': w0[l][None, :], 'a0': a0[l][None, :], 'k_k': k_k[l][None, :],
         'k_a': k_a[l][None, :], 'r_k': r_k[l].reshape(1, RWKV_WIDTH), 'lora': lora}
    if l > 0:
        p['v0'] = v0[l - 1][None, :]
        p['v1'] = v1[l - 1]
        p['v2'] = v2[l - 1]
    p_bf16 = {k_: (v_.astype(BF16) if k_ in ('lora', 'v1', 'v2') else v_) for k_, v_ in p.items()}
    return p, p_bf16


def _wkv_body(r_ref, w_ref, k_ref, v_ref, a_ref, b_ref, s0_ref, g_ref, eye_ref, y_ref, sout_ref, s_s, sb_s, *, nb,
              tc, precise):
    c = pl.program_id(1)

    @pl.when(c == 0)
    def _():
        s_s[...] = s0_ref[...]
        sb_s[...] = s0_ref[...].astype(BF16)

    gmat = g_ref[...]
    eye = eye_ref[...][None]
    eye_b = eye.astype(BF16)
    rows = nb * RWKV_HEAD
    dot = functools.partial(jnp.dot, preferred_element_type=F32)

    groups = [slice(q * GROUP, (q + 1) * GROUP) for q in range(N_GROUPS)]

    def block(red, n):
        return red[n * rows:(n + 1) * rows].reshape(nb, RWKV_HEAD, GROUP)

    def diagonal(yx):
        d = (yx * eye).reshape(nb, RWKV_HEAD // SUBLANES, SUBLANES, GROUP)
        return jnp.sum(jnp.sum(d, axis=1), axis=1, keepdims=True)

    def store_y(t, yx):
        for q, sl in enumerate(groups):
            y_ref[:, pl.ds(t, 1), sl] = diagonal(block(yx, q))

    def step_bf16(t, carry):
        tp = jnp.maximum(t - 1, 0)
        w_t, k_t, v_t, a_t, b_t = (ref[:, pl.ds(t, 1), :] for ref in (w_ref, k_ref, v_ref, a_ref, b_ref))
        a_b, v_b, r_b = a_t.astype(BF16), v_t.astype(BF16), r_ref[:, pl.ds(tp, 1), :].astype(BF16)
        tiles = (nb, RWKV_HEAD // BF16_ROWS, BF16_ROWS, GROUP)
        lhs = []
        for q, sl in enumerate(groups):
            sb = sb_s[q]
            parts = [(sb * a_b[:, :, sl]).reshape(tiles), (eye_b * v_b[:, :, sl]).reshape(tiles),
                     (sb * r_b[:, :, sl]).reshape(tiles)]
            lhs.append(jnp.stack(parts, axis=2).reshape(3 * rows, GROUP))
        red = dot(jnp.concatenate(lhs, axis=0), gmat)
        ys = []
        for q, sl in enumerate(groups):
            rq = red[3 * q * rows:3 * (q + 1) * rows].reshape(tiles[:2] + (3,) + tiles[2:])
            sa, vx = (rq[:, :, j].reshape(nb, RWKV_HEAD, GROUP) for j in range(2))
            s_new = s_s[q] * w_t[:, :, sl] + sa * b_t[:, :, sl] + vx * k_t[:, :, sl]
            s_s[q] = s_new
            sb_s[q] = s_new.astype(BF16)
            ys.append(diagonal(rq[:, :, 2].reshape(nb, RWKV_HEAD, GROUP)))

        @pl.when(t > 0)
        def _():
            for q, sl in enumerate(groups):
                y_ref[:, pl.ds(tp, 1), sl] = ys[q]

        return carry

    def last_y_bf16():
        r_b = r_ref[:, pl.ds(tc - 1, 1), :].astype(BF16)
        lhs = [(sb_s[q] * r_b[:, :, sl]).reshape(rows, GROUP) for q, sl in enumerate(groups)]
        store_y(tc - 1, dot(jnp.concatenate(lhs, axis=0), gmat))

    def head_sums3(xs):
        lhs = []
        for x in xs:
            lhs += list(_split3(x.reshape(rows, GROUP)))
        red = dot(jnp.concatenate(lhs, axis=0), gmat)
        out = []
        for q in range(len(xs)):
            o = 3 * q * rows
            acc = red[o:o + rows] + red[o + rows:o + 2 * rows] + red[o + 2 * rows:o + 3 * rows]
            out.append(acc.reshape(nb, RWKV_HEAD, GROUP))
        return out

    def step_f32(t, carry):
        r_t, w_t, k_t, v_t, a_t, b_t = (ref[:, pl.ds(t, 1), :] for ref in (r_ref, w_ref, k_ref, v_ref, a_ref, b_ref))
        groups = [slice(q * GROUP, (q + 1) * GROUP) for q in range(N_GROUPS)]
        sa = head_sums3([s_s[q] * a_t[:, :, sl] for q, sl in enumerate(groups)])
        vx = head_sums3([eye * v_t[:, :, sl] for sl in groups])
        prods = []
        for q, sl in enumerate(groups):
            s_new = s_s[q] * w_t[:, :, sl] + sa[q] * b_t[:, :, sl] + vx[q] * k_t[:, :, sl]
            s_s[q] = s_new
            prods.append(s_new * r_t[:, :, sl])
        yx = head_sums3(prods)
        for q, sl in enumerate(groups):
            y_ref[:, pl.ds(t, 1), sl] = jnp.sum(yx[q] * eye, axis=1, keepdims=True)
        return carry

    if precise:
        lax.fori_loop(0, tc, step_f32, 0)
    else:
        lax.fori_loop(0, tc, step_bf16, 0)
        last_y_bf16()

    @pl.when(c == pl.num_programs(1) - 1)
    def _():
        sout_ref[...] = s_s[...]


def wkv_scan(r, w, k, v, a, b, s0, nb=4, tc=64, precise=False):
    B, T, W = r.shape
    tc = min(tc, T)
    assert B % nb == 0 and T % tc == 0
    xspec = pl.BlockSpec((nb, tc, W), lambda i, c: (i, c, 0))
    sspec = pl.BlockSpec((N_GROUPS, nb, RWKV_HEAD, GROUP), lambda i, c: (0, i, 0, 0))
    j = jnp.arange(GROUP)
    eye = (jnp.arange(RWKV_HEAD)[:, None] == j[None, :] % RWKV_HEAD).astype(F32)
    return pl.pallas_call(
        functools.partial(_wkv_body, nb=nb, tc=tc, precise=precise),
        grid=(B // nb, T // tc),
        in_specs=[xspec] * 6 + [sspec, pl.BlockSpec((GROUP, GROUP), lambda i, c: (0, 0)),
                                pl.BlockSpec((RWKV_HEAD, GROUP), lambda i, c: (0, 0))],
        out_specs=[xspec, sspec],
        out_shape=[jax.ShapeDtypeStruct((B, T, W), F32), jax.ShapeDtypeStruct(s0.shape, F32)],
        scratch_shapes=[pltpu.VMEM((N_GROUPS, nb, RWKV_HEAD, GROUP), F32),
                        pltpu.VMEM((N_GROUPS, nb, RWKV_HEAD, GROUP), BF16)],
        compiler_params=_params(("arbitrary", "arbitrary")),
        name="wkv_scan",
    )(r, w, k, v, a, b, s0, _head_sum_matrix(), eye)


def _state_to_groups(s):
    B = s.shape[0]
    return s.reshape(B, N_GROUPS, HEADS_PER_GROUP, RWKV_HEAD, RWKV_HEAD).transpose(1, 0, 3, 2, 4).reshape(
        N_GROUPS, B, RWKV_HEAD, GROUP)


def _state_from_groups(s):
    B = s.shape[1]
    return s.reshape(N_GROUPS, B, RWKV_HEAD, HEADS_PER_GROUP, RWKV_HEAD).transpose(1, 0, 3, 2, 4).reshape(
        B, N_RWKV_HEADS, RWKV_HEAD, RWKV_HEAD)


def _rwkv_post_body(y_ref, bonus_ref, g_ref, lw_ref, lb_ref, gm_ref, o_ref):
    gmat = gm_ref[...]
    y = y_ref[...]
    mean = _head_sum(y, gmat) * (1.0 / RWKV_HEAD)
    d = y - mean
    var = _head_sum(d * d, gmat) * (1.0 / RWKV_HEAD)
    yn = d * lax.rsqrt(var + GN_EPS) * lw_ref[...] + lb_ref[...]
    o_ref[...] = ((yn + bonus_ref[...]) * g_ref[...]).astype(o_ref.dtype)


def rwkv_post(y, bonus, g, ln_w, ln_b, out_dtype, tr=256):
    n, W = y.shape
    tr = min(tr, n)
    assert n % tr == 0
    spec = pl.BlockSpec((tr, W), lambda i: (i, 0))
    vec = pl.BlockSpec((1, W), lambda i: (0, 0))
    return pl.pallas_call(
        _rwkv_post_body, grid=(n // tr,),
        in_specs=[spec, spec, spec, vec, vec, pl.BlockSpec((GROUP, GROUP), lambda i: (0, 0))],
        out_specs=spec, out_shape=jax.ShapeDtypeStruct((n, W), out_dtype),
        compiler_params=_params(("parallel",)),
        name="rwkv_post",
    )(y, bonus, g, ln_w[None, :], ln_b[None, :], _head_sum_matrix())


def _rwkv_mixer(u_rw, B, T, prev_rows, s0_groups, p, v_first, ln_w, ln_b, tr, precise):
    r, w, k, v, a, b, g, bonus = rwkv_prep(u_rw, 0, B, T, prev_rows, p, v_first, tr)
    sh = lambda t: t.reshape(B, T, RWKV_WIDTH)
    y, s_out = wkv_scan(sh(r), sh(w), sh(k), sh(v), sh(a), sh(b), s0_groups, precise=precise)
    out = rwkv_post(y.reshape(B * T, RWKV_WIDTH), bonus, g, ln_w, ln_b, F32 if precise else BF16)
    return out, _state_from_groups(s_out), v


TM = 1024
TM_MOE = 512
TC = 256
TR_PREP = 128
TN_SMALL = 512


def kernel(x_prompt, x_sample, state_shift, state_wkv, cache_win_k, cache_win_v, attn_norm_w, w_in, q_norm_w,
           k_norm_w, attn_sinks, shift_mu, w0, w2, a0, a2, g2, k_k, k_a, r_k, ln_x_w, ln_x_b, v0, v1, v2, w_out,
           ffn_norm_w, dense_w1, dense_w3, dense_w2, router_w, moe_w1, moe_w3, moe_w2):
    Bp, Tp, D = x_prompt.shape
    Bs, Ts, _ = x_sample.shape
    assert Ts == 1 and cache_win_k.shape[2] == WINDOW and Tp % BLOCK == 0
    Mp, Ms = Bp * Tp, Bs * Ts
    assert Mp % TM == 0 and Mp % TC == 0
    xp = x_prompt.reshape(Mp, D)
    xs = x_sample.reshape(Ms, D)
    cache_k = cache_win_k.reshape(DEPTH, Bs, WINDOW, KV_WIDTH)
    cache_v = cache_win_v.reshape(DEPTH, Bs, WINDOW, KV_WIDTH)
    n_moe_tiles = (TOP_K * (Mp + Ms)) // TM_MOE + N_EXPERTS
    zero_state = jnp.zeros((N_GROUPS, Bp, RWKV_HEAD, GROUP), F32)
    vf_p = vf_s = None
    outs = {k_: [] for k_ in ('p_row', 'p_wkv', 'p_k', 'p_v', 's_row', 's_wkv', 's_k', 's_v')}
    for l in range(DEPTH):
        te, mode = _dense_tiles(Mp, TM, l)
        p_f32, p_bf16 = _rwkv_params(l, shift_mu, w0, w2, a0, a2, g2, k_k, k_a, r_k, v0, v1, v2)
        an = attn_norm_w[l][None, :]
        h = rmsnorm_bf16(xp, an)
        u_att = matmul_proj(h, w_in, te, mode, n_lo=0, n_cols=ATT_PROJ, tm=TM, tn=512)
        u_rw = matmul_proj(h, w_in, te, mode, n_lo=ATT_PROJ, n_cols=SHIFT_WIDTH, tm=TM, tn=512)
        att_p, knew_p = attention_prompt(u_att, Bp, Tp, q_norm_w[l], k_norm_w[l], attn_sinks[l])
        rw_p, S_p, v_p_first = _rwkv_mixer(u_rw, Bp, Tp, None, zero_state, p_bf16, vf_p, ln_x_w[l], ln_x_b[l],
                                           TR_PREP, False)
        xp = matmul_residual(jnp.concatenate([att_p, rw_p], axis=1), w_out, xp, te, mode, tm=TM, tn=512, tk=D)
        us_att = small_matmul('proj', xs, [w_in], l, tn=TN_SMALL, norm_w=an, n_lo=0, n_cols=ATT_PROJ)
        us_rw = small_matmul('proj', xs, [w_in], l, tn=TN_SMALL, norm_w=an, n_lo=ATT_PROJ, n_cols=SHIFT_WIDTH)
        att_s, k_s, v_s = attention_sample(us_att, 0, Bs, cache_k, cache_v, l, q_norm_w[l], k_norm_w[l],
                                           attn_sinks[l])
        rw_s, S_s, v_s_first = _rwkv_mixer(us_rw, Bs, Ts, state_shift[l], _state_to_groups(state_wkv[l]), p_f32,
                                           vf_s, ln_x_w[l], ln_x_b[l], Bs, True)
        xs = small_matmul('res', jnp.concatenate([att_s, rw_s], axis=1), [w_out], l, tn=TN_SMALL, res=xs)
        if l == 0:
            vf_p, vf_s = v_p_first, v_s_first
        fn = ffn_norm_w[l][None, :]
        if l % 2 == 0:
            te, mode = _dense_tiles(Mp, TM, l // 2)
            h = rmsnorm_bf16(xp, fn)
            g = matmul_swiglu_in(h, dense_w1, dense_w3, te, mode, tm=TM, tn=512)
            xp = matmul_residual(g, dense_w2, xp, te, mode, tm=TM, tn=1024, tk=D_FF // 4)
            gs = small_matmul('swiglu', xs, [dense_w1, dense_w3], l // 2, tn=TN_SMALL // 2, norm_w=fn)
            xs = small_matmul('res', gs, [dense_w2], l // 2, tn=TN_SMALL // 4, res=xs)
        else:
            router_pad = jnp.pad(router_w[l // 2], ((0, 0), (0, LANES - N_EXPERTS)))
            idx_p, gates_p = route_top2(xp, fn, router_pad)
            idx_s, gates_s = route_top2(xs, fn, router_pad, tr=Ms)
            te, mode, src, gate_rows, slot = _route_tables(
                jnp.concatenate([idx_p, idx_s], axis=0), jnp.concatenate([gates_p, gates_s], axis=0),
                TM_MOE, n_moe_tiles)
            rows = gather_rmsnorm_bf16(xp, xs, fn, src, tg=TC)
            g = matmul_swiglu_in(rows, moe_w1[l // 2], moe_w3[l // 2], te, mode, tm=TM_MOE, tn=512, parts=True)
            ys = matmul_residual(g, moe_w2[l // 2], jnp.zeros((rows.shape[0], D), F32), te, mode,
                                 tm=TM_MOE, tn=1024, tk=D_FF // 4, parts=True, scale=gate_rows)
            xp = combine_top2(xp, ys, slot[:Mp, 0], slot[:Mp, 1], tc=TC)
            xs = combine_top2(xs, ys, slot[Mp:, 0], slot[Mp:, 1], tc=Ms)
        u_att_p = u_att.reshape(Bp, Tp, ATT_PROJ)
        outs['p_row'].append(u_rw.reshape(Bp, Tp, SHIFT_WIDTH)[:, Tp - 1])
        outs['p_wkv'].append(S_p)
        outs['p_k'].append(knew_p.reshape(Bp, WINDOW, N_KV_HEADS, HEAD_DIM))
        outs['p_v'].append(u_att_p[:, Tp - WINDOW:, ATT_WIDTH + KV_WIDTH:].reshape(Bp, WINDOW, N_KV_HEADS, HEAD_DIM))
        outs['s_row'].append(us_rw)
        outs['s_wkv'].append(S_s)
        outs['s_k'].append(k_s.reshape(Bs, WINDOW, N_KV_HEADS, HEAD_DIM))
        outs['s_v'].append(v_s.reshape(Bs, WINDOW, N_KV_HEADS, HEAD_DIM))
    return (xp.reshape(Bp, Tp, D), xs.reshape(Bs, Ts, D),
            jnp.stack(outs['p_row']), jnp.stack(outs['p_wkv']), jnp.stack(outs['p_k']), jnp.stack(outs['p_v']),
            jnp.stack(outs['s_row']), jnp.stack(outs['s_wkv']), jnp.stack(outs['s_k']), jnp.stack(outs['s_v']))
```

```python
import functools
import math

import jax
import jax.numpy as jnp
from jax import lax
from jax.experimental import pallas as pl
from jax.experimental.pallas import tpu as pltpu

F32 = jnp.float32
BF16 = jnp.bfloat16
I32 = jnp.int32

D_MODEL = 4096
DEPTH = 2
PAST_LEN = 16384
HEAD_DIM = 64
ATT_WIDTH = D_MODEL // 2
N_Q_HEADS = ATT_WIDTH // HEAD_DIM
N_KV_HEADS = 8
Q_PER_KV = N_Q_HEADS // N_KV_HEADS
KV_WIDTH = N_KV_HEADS * HEAD_DIM
ATT_PROJ = ATT_WIDTH + 2 * KV_WIDTH
WINDOW = 128
BLOCK = 128
ROPE_DIM = HEAD_DIM // 4
ROPE_THETA = 500000.0
ATT_SCALE = HEAD_DIM ** -0.5
RWKV_WIDTH = D_MODEL - ATT_WIDTH
RWKV_HEAD = 64
N_RWKV_HEADS = RWKV_WIDTH // RWKV_HEAD
DECAY_LORA = 96
AAA_LORA = 96
GATE_LORA = 256
MV_LORA = 64
LORA_IN = DECAY_LORA + AAA_LORA + GATE_LORA
SHIFT_WIDTH = 3 * RWKV_WIDTH + LORA_IN
D_FF = 14336
N_EXPERTS = 8
TOP_K = 2
RMS_EPS = 1e-5
GN_EPS = 64e-5

VMEM_LIMIT_BYTES = 58 * 2 ** 20
LANES = 128
SUBLANES = 8
BF16_ROWS = 2 * SUBLANES
LORA_PAD = 4 * LANES
GROUP = 2 * LANES
N_GROUPS = RWKV_WIDTH // GROUP
HEADS_PER_GROUP = GROUP // RWKV_HEAD

MODE_SKIP, MODE_FULL = 0, 1
MODE_PART = 8
PART_ROWS = LANES
DMA_UNROLL = 8


def _cdiv(a, b):
    return -(-a // b)


def _params(sem):
    return pltpu.CompilerParams(dimension_semantics=sem, vmem_limit_bytes=VMEM_LIMIT_BYTES)


def _split3(x):
    hi = x.astype(BF16)
    r1 = x - hi.astype(F32)
    mid = r1.astype(BF16)
    lo = (r1 - mid.astype(F32)).astype(BF16)
    return hi, mid, lo


def _dot_exact_rhs(x, rhs_bf16):
    hi, mid, lo = _split3(x)
    d = functools.partial(jnp.dot, preferred_element_type=F32)
    return d(hi, rhs_bf16) + d(mid, rhs_bf16) + d(lo, rhs_bf16)


def _rmsnorm_body(x_ref, w_ref, o_ref):
    x = x_ref[...]
    ms = jnp.mean(x * x, axis=-1, keepdims=True)
    o_ref[...] = (x * lax.rsqrt(ms + RMS_EPS) * w_ref[...]).astype(o_ref.dtype)


def rmsnorm_bf16(x, w, tr=512):
    M, D = x.shape
    return pl.pallas_call(
        _rmsnorm_body,
        grid=(_cdiv(M, tr),),
        in_specs=[pl.BlockSpec((tr, D), lambda i: (i, 0)), pl.BlockSpec((1, D), lambda i: (0, 0))],
        out_specs=pl.BlockSpec((tr, D), lambda i: (i, 0)),
        out_shape=jax.ShapeDtypeStruct((M, D), BF16),
        compiler_params=_params(("parallel",)),
        name="rmsnorm_bf16",
    )(x, w)


def _router_body(x_ref, w_ref, r_ref, idx_ref, gate_ref):
    x = x_ref[...]
    ms = jnp.mean(x * x, axis=-1, keepdims=True)
    h = x * lax.rsqrt(ms + RMS_EPS) * w_ref[...]
    logits = _dot3(h, r_ref[...])
    lane = lax.broadcasted_iota(I32, logits.shape, 1)
    neg = jnp.float32(-jnp.inf)
    l1 = jnp.where(lane < N_EXPERTS, logits, neg)
    m1 = jnp.max(l1, axis=-1, keepdims=True)
    i1 = jnp.min(jnp.where(l1 == m1, lane, LANES), axis=-1, keepdims=True)
    l2 = jnp.where(lane == i1, neg, l1)
    m2 = jnp.max(l2, axis=-1, keepdims=True)
    i2 = jnp.min(jnp.where(l2 == m2, lane, LANES), axis=-1, keepdims=True)
    e2 = jnp.exp(m2 - m1)
    den = 1.0 + e2
    idx_ref[...] = jnp.where(lane == 0, i1, jnp.where(lane == 1, i2, 0))
    gate_ref[...] = jnp.where(lane == 0, 1.0 / den, jnp.where(lane == 1, e2 / den, 0.0))


def route_top2(x, w, router_pad, tr=512):
    M, D = x.shape
    idx, gate = pl.pallas_call(
        _router_body,
        grid=(_cdiv(M, tr),),
        in_specs=[pl.BlockSpec((tr, D), lambda i: (i, 0)), pl.BlockSpec((1, D), lambda i: (0, 0)),
                  pl.BlockSpec((D, LANES), lambda i: (0, 0))],
        out_specs=[pl.BlockSpec((tr, LANES), lambda i: (i, 0)), pl.BlockSpec((tr, LANES), lambda i: (i, 0))],
        out_shape=[jax.ShapeDtypeStruct((M, LANES), I32), jax.ShapeDtypeStruct((M, LANES), F32)],
        compiler_params=_params(("parallel",)),
        name="route_top2",
    )(x, w, router_pad)
    return idx[:, :TOP_K], gate[:, :TOP_K]


def _first_of_weight(te_ref, m):
    prev = te_ref[jnp.maximum(m - 1, 0)]
    return jnp.logical_or(m == 0, te_ref[m] != prev)


def _for_rows(mode_ref, m, parts, fn, o_ref):
    mode = mode_ref[m]
    pl.when(mode == MODE_FULL)(lambda: fn(slice(None)))

    def zero(sl):
        o_ref[sl, :] = jnp.zeros(o_ref[sl, :].shape, o_ref.dtype)

    pl.when(mode == MODE_SKIP)(lambda: zero(slice(None)))
    if parts:
        tm = o_ref.shape[0]
        assert tm % PART_ROWS == 0
        for c in range(tm // PART_ROWS):
            sl = slice(c * PART_ROWS, (c + 1) * PART_ROWS)
            pl.when(jnp.logical_and(mode >= MODE_PART, c < mode - MODE_PART))(lambda sl=sl: fn(sl))
            pl.when(jnp.logical_and(mode >= MODE_PART, c >= mode - MODE_PART))(lambda sl=sl: zero(sl))


def _proj_body(te_ref, mode_ref, a_ref, w_ref, o_ref, wb_ref, *, parts):
    m = pl.program_id(1)
    pl.when(_first_of_weight(te_ref, m))(lambda: wb_ref.__setitem__(Ellipsis, w_ref[...].astype(BF16)))

    def rows(sl):
        o_ref[sl, :] = jnp.dot(a_ref[sl, :], wb_ref[...], preferred_element_type=F32)

    _for_rows(mode_ref, m, parts, rows, o_ref)


def matmul_proj(a, w, te, mode, *, n_lo, n_cols, tm, tn, parts=False):
    M, K = a.shape
    nb0 = n_lo // tn
    assert nb0 * tn == n_lo
    grid = (_cdiv(n_cols, tn), _cdiv(M, tm))
    return pl.pallas_call(
        functools.partial(_proj_body, parts=parts),
        grid_spec=pltpu.PrefetchScalarGridSpec(
            num_scalar_prefetch=2, grid=grid,
            in_specs=[pl.BlockSpec((tm, K), lambda n, m, te, md: (m, 0)),
                      pl.BlockSpec((None, K, tn), lambda n, m, te, md: (te[m], 0, nb0 + n))],
            out_specs=pl.BlockSpec((tm, tn), lambda n, m, te, md: (m, n)),
            scratch_shapes=[pltpu.VMEM((K, tn), BF16)]),
        out_shape=jax.ShapeDtypeStruct((M, n_cols), F32),
        compiler_params=_params(("arbitrary", "arbitrary")),
        name="matmul_proj",
    )(te, mode, a, w)


def _next_weight(te):
    big = jnp.int32(2 ** 30)
    nxt = jnp.min(jnp.where(te[None, :] > te[:, None], te[None, :], big), axis=1)
    return jnp.where(nxt == big, -1, nxt).astype(I32)


def _stream_weights(te_ref, nxt_ref, w_hbms, wf_refs, wb_refs, sems, *, k_lo, tk, tn):
    n, m = pl.program_id(0), pl.program_id(1)

    def copy(i, e, nn):
        cols = pl.ds(pl.multiple_of(nn * tn, tn), tn)
        return pltpu.make_async_copy(w_hbms[i].at[e, pl.ds(k_lo, tk), cols], wf_refs[i], sems.at[i])

    @pl.when(_first_of_weight(te_ref, m))
    def _():
        e = te_ref[m]

        @pl.when(jnp.logical_and(n == 0, m == 0))
        def _():
            for i in range(len(w_hbms)):
                copy(i, e, n).start()

        for i in range(len(w_hbms)):
            copy(i, e, n).wait()
            wb_refs[i][...] = wf_refs[i][...].astype(BF16)
        nxt = nxt_ref[m]
        same_sweep = nxt >= 0
        e2 = jnp.where(same_sweep, nxt, te_ref[0])
        n2 = jnp.where(same_sweep, n, n + 1)

        @pl.when(jnp.logical_or(same_sweep, n + 1 < pl.num_programs(0)))
        def _():
            for i in range(len(w_hbms)):
                copy(i, e2, n2).start()


def _ffn1_body(te_ref, mode_ref, nxt_ref, a_ref, w1_hbm, w3_hbm, o_ref, wf1_ref, wf3_ref, wb1_ref, wb3_ref, sems,
               *, parts, tn):
    m = pl.program_id(1)
    _stream_weights(te_ref, nxt_ref, (w1_hbm, w3_hbm), (wf1_ref, wf3_ref), (wb1_ref, wb3_ref), sems,
                    k_lo=0, tk=wf1_ref.shape[0], tn=tn)

    def rows(sl):
        a = a_ref[sl, :]
        u1 = jnp.dot(a, wb1_ref[...], preferred_element_type=F32)
        u3 = jnp.dot(a, wb3_ref[...], preferred_element_type=F32)
        o_ref[sl, :] = (u1 * jax.nn.sigmoid(u1) * u3).astype(o_ref.dtype)

    _for_rows(mode_ref, m, parts, rows, o_ref)


def matmul_swiglu_in(a, w1, w3, te, mode, *, tm, tn, parts=False):
    M, K = a.shape
    F = w1.shape[-1]
    assert F % tn == 0
    grid = (F // tn, _cdiv(M, tm))
    wspec = pl.BlockSpec(memory_space=pl.ANY)
    return pl.pallas_call(
        functools.partial(_ffn1_body, parts=parts, tn=tn),
        grid_spec=pltpu.PrefetchScalarGridSpec(
            num_scalar_prefetch=3, grid=grid,
            in_specs=[pl.BlockSpec((tm, K), lambda n, m, te, md, nx: (m, 0)), wspec, wspec],
            out_specs=pl.BlockSpec((tm, tn), lambda n, m, te, md, nx: (m, n)),
            scratch_shapes=[pltpu.VMEM((K, tn), F32), pltpu.VMEM((K, tn), F32),
                            pltpu.VMEM((K, tn), BF16), pltpu.VMEM((K, tn), BF16),
                            pltpu.SemaphoreType.DMA((2,))]),
        out_shape=jax.ShapeDtypeStruct((M, F), BF16),
        compiler_params=_params(("arbitrary", "arbitrary")),
        name="matmul_swiglu_in",
    )(te, mode, _next_weight(te), a, w1, w3)


def _res_body(te_ref, mode_ref, nxt_ref, a_ref, w_hbm, *rest, parts, has_res, scaled, k_lo, tn):
    rest = list(rest)
    res_ref = rest.pop(0) if has_res else None
    scale_ref = rest.pop(0) if scaled else None
    o_ref, wf_ref, wb_ref, sems = rest
    m = pl.program_id(1)
    _stream_weights(te_ref, nxt_ref, (w_hbm,), (wf_ref,), (wb_ref,), sems, k_lo=k_lo, tk=wf_ref.shape[0], tn=tn)

    def rows(sl):
        d = jnp.dot(a_ref[sl, :], wb_ref[...], preferred_element_type=F32)
        if scaled:
            d = d * scale_ref[sl, :]
        o_ref[sl, :] = res_ref[sl, :] + d if has_res else d

    _for_rows(mode_ref, m, parts, rows, o_ref)


def matmul_residual(a, w, res, te, mode, *, tm, tn, tk, parts=False, scale=None, in_place=True):
    M, K = a.shape
    N = w.shape[-1]
    assert K % tk == 0 and N % tn == 0
    grid = (N // tn, _cdiv(M, tm))
    nxt = _next_weight(te)
    for kb in range(K // tk):
        in_specs = [pl.BlockSpec((tm, tk), lambda n, m, te, md, nx, kb=kb: (m, kb)),
                    pl.BlockSpec(memory_space=pl.ANY)]
        args = [te, mode, nxt, a, w]
        aliases = {}
        if res is not None:
            if in_place or kb > 0:
                aliases = {len(args): 0}
            in_specs.append(pl.BlockSpec((tm, tn), lambda n, m, te, md, nx: (m, n)))
            args.append(res)
        if scale is not None:
            in_specs.append(pl.BlockSpec((tm, 1), lambda n, m, te, md, nx: (m, 0)))
            args.append(scale)
        res = pl.pallas_call(
            functools.partial(_res_body, parts=parts, has_res=res is not None, scaled=scale is not None,
                              k_lo=kb * tk, tn=tn),
            grid_spec=pltpu.PrefetchScalarGridSpec(
                num_scalar_prefetch=3, grid=grid, in_specs=in_specs,
                out_specs=pl.BlockSpec((tm, tn), lambda n, m, te, md, nx: (m, n)),
                scratch_shapes=[pltpu.VMEM((tk, tn), F32), pltpu.VMEM((tk, tn), BF16),
                                pltpu.SemaphoreType.DMA((1,))]),
            out_shape=jax.ShapeDtypeStruct((M, N), F32),
            input_output_aliases=aliases,
            compiler_params=_params(("arbitrary", "arbitrary")),
            name="matmul_residual",
        )(*args)
    return res


def _dense_tiles(M, tm, e):
    assert M % tm == 0
    return jnp.full((M // tm,), e, I32), jnp.full((M // tm,), MODE_FULL, I32)


_hp_dot = functools.partial(jnp.dot, preferred_element_type=F32, precision=lax.Precision.HIGHEST)


def _split2(x):
    hi = x.astype(BF16)
    return hi, (x - hi.astype(F32)).astype(BF16)


def _dot3(a, w):
    a_hi, a_lo = _split2(a)
    w_hi, w_lo = _split2(w)
    r = a.shape[0]
    first = jnp.dot(jnp.concatenate([a_hi, a_lo], axis=0), w_hi, preferred_element_type=F32)
    return first[:r] + first[r:] + jnp.dot(a_hi, w_lo, preferred_element_type=F32)


def _small_body(*refs, kind, normed):
    it = iter(refs)
    x_ref = next(it)
    nw_ref = next(it) if normed else None
    a = x_ref[...]
    if normed:
        a = a * lax.rsqrt(jnp.mean(a * a, axis=-1, keepdims=True) + RMS_EPS) * nw_ref[...]
    if kind == 'swiglu':
        w1_ref, w3_ref, o_ref = it
        u1 = _dot3(a, w1_ref[...])
        u3 = _dot3(a, w3_ref[...])
        o_ref[...] = u1 * jax.nn.sigmoid(u1) * u3
    elif kind == 'res':
        w_ref, res_ref, o_ref = it
        o_ref[...] = res_ref[...] + _dot3(a, w_ref[...])
    else:
        w_ref, o_ref = it
        o_ref[...] = _dot3(a, w_ref[...])


def small_matmul(kind, x, ws, e, *, tn, norm_w=None, res=None, n_lo=0, n_cols=None):
    R, K = x.shape
    N = ws[0].shape[-1]
    n_cols = N - n_lo if n_cols is None else n_cols
    nb0 = n_lo // tn
    assert nb0 * tn == n_lo
    normed = norm_w is not None
    row = lambda w: pl.BlockSpec((R, w), lambda n: (0, 0))
    in_specs, args = [row(K)], [x]
    if normed:
        in_specs.append(pl.BlockSpec((1, K), lambda n: (0, 0)))
        args.append(norm_w)
    for w in ws:
        in_specs.append(pl.BlockSpec((None, K, tn), lambda n: (e, 0, nb0 + n)))
        args.append(w)
    ospec = pl.BlockSpec((R, tn), lambda n: (0, n))
    aliases = {}
    if kind == 'res':
        in_specs.append(ospec)
        aliases = {len(args): 0}
        args.append(res)
    return pl.pallas_call(
        functools.partial(_small_body, kind=kind, normed=normed),
        grid=(_cdiv(n_cols, tn),), in_specs=in_specs, out_specs=ospec,
        out_shape=jax.ShapeDtypeStruct((R, n_cols), F32),
        input_output_aliases=aliases,
        compiler_params=_params(("arbitrary",)),
        name="small_matmul_" + kind,
    )(*args)


def _gather_norm_body(src_ref, xp_hbm, xs_hbm, w_ref, o_ref, buf, sem, *, tg, mp):
    def issue(r, c):
        t = src_ref[r]
        dst = buf.at[pl.ds(r, 1), :]
        pl.when(t < mp)(lambda: pltpu.make_async_copy(xp_hbm.at[pl.ds(t, 1), :], dst, sem).start())
        pl.when(t >= mp)(lambda: pltpu.make_async_copy(xs_hbm.at[pl.ds(t - mp, 1), :], dst, sem).start())
        return c

    lax.fori_loop(0, tg, issue, 0, unroll=DMA_UNROLL)

    def wait(r, c):
        pltpu.make_async_copy(xp_hbm.at[pl.ds(0, 1), :], buf.at[pl.ds(r, 1), :], sem).wait()
        return c

    lax.fori_loop(0, tg, wait, 0, unroll=DMA_UNROLL)
    x = buf[...]
    ms = jnp.mean(x * x, axis=-1, keepdims=True)
    o_ref[...] = (x * lax.rsqrt(ms + RMS_EPS) * w_ref[...]).astype(o_ref.dtype)


def gather_rmsnorm_bf16(xp, xs, w, src, tg=256):
    Mp, D = xp.shape
    R = src.shape[0]
    assert R % tg == 0
    return pl.pallas_call(
        functools.partial(_gather_norm_body, tg=tg, mp=Mp),
        grid=(R // tg,),
        in_specs=[pl.BlockSpec((tg,), lambda i: (i,), memory_space=pltpu.SMEM),
                  pl.BlockSpec(memory_space=pl.ANY), pl.BlockSpec(memory_space=pl.ANY),
                  pl.BlockSpec((1, D), lambda i: (0, 0))],
        out_specs=pl.BlockSpec((tg, D), lambda i: (i, 0)),
        out_shape=jax.ShapeDtypeStruct((R, D), BF16),
        scratch_shapes=[pltpu.VMEM((tg, D), F32), pltpu.SemaphoreType.DMA(())],
        compiler_params=_params(("arbitrary",)),
        name="gather_rmsnorm_bf16",
    )(src, xp, xs, w)


def _combine_body(s0_ref, s1_ref, x_ref, y_hbm, o_ref, buf0, buf1, sem, *, tc):
    def issue(r, c):
        pltpu.make_async_copy(y_hbm.at[pl.ds(s0_ref[r], 1), :], buf0.at[pl.ds(r, 1), :], sem).start()
        pltpu.make_async_copy(y_hbm.at[pl.ds(s1_ref[r], 1), :], buf1.at[pl.ds(r, 1), :], sem).start()
        return c

    lax.fori_loop(0, tc, issue, 0, unroll=DMA_UNROLL)

    def wait(r, c):
        pltpu.make_async_copy(y_hbm.at[pl.ds(0, 1), :], buf0.at[pl.ds(r, 1), :], sem).wait()
        pltpu.make_async_copy(y_hbm.at[pl.ds(0, 1), :], buf1.at[pl.ds(r, 1), :], sem).wait()
        return c

    lax.fori_loop(0, tc, wait, 0, unroll=DMA_UNROLL)
    o_ref[...] = x_ref[...] + (buf0[...] + buf1[...])


def combine_top2(x, ys, slot0, slot1, tc=256):
    M, D = x.shape
    nt = _cdiv(M, tc)
    assert slot0.shape[0] == nt * tc
    sspec = pl.BlockSpec((tc,), lambda i: (i,), memory_space=pltpu.SMEM)
    return pl.pallas_call(
        functools.partial(_combine_body, tc=tc),
        grid=(nt,),
        in_specs=[sspec, sspec, pl.BlockSpec((tc, D), lambda i: (i, 0)), pl.BlockSpec(memory_space=pl.ANY)],
        out_specs=pl.BlockSpec((tc, D), lambda i: (i, 0)),
        out_shape=jax.ShapeDtypeStruct((M, D), F32),
        scratch_shapes=[pltpu.VMEM((tc, D), F32), pltpu.VMEM((tc, D), F32), pltpu.SemaphoreType.DMA(())],
        compiler_params=_params(("arbitrary",)),
        name="combine_top2",
    )(slot0, slot1, x, ys)


def _route_tables(idx, gates, tmg, n_tiles):
    M = idx.shape[0]
    e = idx.reshape(-1)
    onehot = (e[:, None] == jnp.arange(N_EXPERTS, dtype=I32)[None, :]).astype(I32)
    csum = jnp.cumsum(onehot, axis=0)
    rank = jnp.take_along_axis(csum, e[:, None], axis=1)[:, 0] - 1
    counts = csum[-1]
    tiles_e = (counts + tmg - 1) // tmg
    tile_end = jnp.cumsum(tiles_e)
    tile_start = tile_end - tiles_e
    slot = tile_start[e] * tmg + rank
    tiles = jnp.arange(n_tiles, dtype=I32)
    te_raw = jnp.sum((tiles[:, None] >= tile_end[None, :]).astype(I32), axis=1)
    total = tile_end[-1]
    valid = tiles < total
    te_last = jnp.sum((total - 1 >= tile_end).astype(I32))
    te = jnp.where(valid, te_raw, te_last).astype(I32)
    tile_rows = jnp.clip(counts[te] - (tiles - tile_start[te]) * tmg, 0, tmg)
    parts = (tile_rows + PART_ROWS - 1) // PART_ROWS
    mode = jnp.where(valid, jnp.where(parts * PART_ROWS >= tmg, MODE_FULL, MODE_PART + parts), MODE_SKIP)
    mode = mode.astype(I32)
    rows = n_tiles * tmg
    src = jnp.zeros((rows,), I32).at[slot].set(jnp.arange(2 * M, dtype=I32) // 2)
    gate_s = jnp.zeros((rows,), F32).at[slot].set(gates.reshape(-1))
    return te, mode, src, gate_s[:, None], slot.reshape(M, TOP_K)


def _rope_tables(pos):
    half = ROPE_DIM // 2
    inv_freq = jnp.exp(-math.log(ROPE_THETA) * 2.0 * jnp.arange(half, dtype=F32) / ROPE_DIM)
    ang = pos.astype(F32)[:, None] * inv_freq[None, :]
    cos, sin = jnp.cos(ang), jnp.sin(ang)
    T = pos.shape[0]
    rest = HEAD_DIM - ROPE_DIM
    ctab = jnp.concatenate([cos, cos, jnp.ones((T, rest), F32)], axis=1)
    stab = jnp.concatenate([-sin, sin, jnp.zeros((T, rest), F32)], axis=1)
    d = jnp.arange(HEAD_DIM)
    partner = jnp.where(d < half, d + half, jnp.where(d < ROPE_DIM, d - half, d))
    swap = (d[:, None] == partner[None, :]).astype(BF16)
    return ctab, stab, swap


def _head_norm_rope(x, nw, ctab, stab, swap):
    y = x * lax.rsqrt(jnp.mean(x * x, axis=-1, keepdims=True) + RMS_EPS) * nw
    return y * ctab + _dot_exact_rhs(y, swap) * stab


def _norm_rope_wide(x, nw, ctab, stab, gmat):
    outs = []
    lane = lax.broadcasted_iota(I32, (x.shape[0], GROUP), 1) % HEAD_DIM
    half = ROPE_DIM // 2
    for q in range(x.shape[1] // GROUP):
        xg = x[:, q * GROUP:(q + 1) * GROUP]
        sq = xg * xg
        hi = sq.astype(BF16)
        lo = (sq - hi.astype(F32)).astype(BF16)
        ss = jnp.dot(hi, gmat, preferred_element_type=F32) + jnp.dot(lo, gmat, preferred_element_type=F32)
        y = xg * lax.rsqrt(ss * (1.0 / HEAD_DIM) + RMS_EPS) * nw
        partner = jnp.where(lane < half, pltpu.roll(y, GROUP - half, axis=1), pltpu.roll(y, half, axis=1))
        outs.append(y * ctab + partner * stab)
    return outs


def _attn_prompt_body(q_ref, kc_ref, kp_ref, vc_ref, vp_ref, cc_ref, sc_ref, cp_ref, sp_ref, qn_ref, kn_ref,
                      sink_ref, g_ref, o_ref, knew_ref):
    i = pl.program_id(1)
    gmat = g_ref[...]
    cc, sc, cp, sp = cc_ref[...], sc_ref[...], cp_ref[...], sp_ref[...]
    qn, kn = qn_ref[...], kn_ref[...]
    R = Q_PER_KV * BLOCK
    row = lax.broadcasted_iota(I32, (R, 2 * BLOCK), 0)
    col = lax.broadcasted_iota(I32, (R, 2 * BLOCK), 1)
    rel = (row % BLOCK) - col + BLOCK
    mask = (rel >= 0) & (rel <= WINDOW) & jnp.logical_not((i == 0) & (col < BLOCK))
    rowh = lax.broadcasted_iota(I32, (R, 1), 0) // BLOCK
    headmask = (lax.broadcasted_iota(I32, (R, GROUP), 0) // BLOCK
                == lax.broadcasted_iota(I32, (R, GROUP), 1) // HEAD_DIM)
    lane_lo = lax.broadcasted_iota(I32, (2 * BLOCK, LANES), 1) < HEAD_DIM
    heads_per_group = GROUP // HEAD_DIM

    q_rot = _norm_rope_wide(q_ref[...], qn, cc, sc, gmat)
    k_cur = _norm_rope_wide(kc_ref[...], kn, cc, sc, gmat)
    k_prev = _norm_rope_wide(kp_ref[...], kn, cp, sp, gmat)
    for j in range(KV_WIDTH // GROUP):
        knew_ref[:, j * GROUP:(j + 1) * GROUP] = k_cur[j]

    def spread(x, g):
        other = pltpu.roll(x, HEAD_DIM, axis=1)
        one = jnp.where(lane_lo, x, other) if g % 2 == 0 else jnp.where(lane_lo, other, x)
        return jnp.concatenate([one, one], axis=1)

    for g in range(N_KV_HEADS):
        j, m = g // heads_per_group, (g % heads_per_group) // 2
        lanes = slice(m * LANES, (m + 1) * LANES)
        kcat = jnp.concatenate([k_prev[j][:, lanes], k_cur[j][:, lanes]], axis=0)
        c0 = g // 2 * LANES
        vcat = jnp.concatenate([vp_ref[:, c0:c0 + LANES], vc_ref[:, c0:c0 + LANES]], axis=0)
        kx = spread(kcat, g).astype(BF16)
        vx = spread(vcat, g).astype(BF16)
        qbd = jnp.where(headmask, jnp.concatenate([q_rot[g]] * Q_PER_KV, axis=0), 0.0).astype(BF16)
        s = lax.dot_general(qbd, kx, (((1,), (1,)), ((), ())), preferred_element_type=F32) * ATT_SCALE
        s = jnp.where(mask, s, -jnp.inf)
        sink = jnp.zeros((R, 1), F32)
        for h in range(Q_PER_KV):
            sink = jnp.where(rowh == h, sink_ref[g * Q_PER_KV + h], sink)
        mx = jnp.maximum(jnp.max(s, axis=-1, keepdims=True), sink)
        p = jnp.exp(s - mx)
        denom = jnp.sum(p, axis=-1, keepdims=True) + jnp.exp(sink - mx)
        o = jnp.dot((p / denom).astype(BF16), vx, preferred_element_type=F32)
        o = jnp.where(headmask, o, 0.0)
        out = o[0:BLOCK]
        for h in range(1, Q_PER_KV):
            out = out + o[h * BLOCK:(h + 1) * BLOCK]
        o_ref[:, g * GROUP:(g + 1) * GROUP] = out.astype(o_ref.dtype)


def attention_prompt(u_att, B, T, q_norm, k_norm, sinks):
    nb = T // BLOCK
    reps = GROUP // HEAD_DIM
    ctab, stab, _ = _rope_tables(jnp.arange(T, dtype=I32))
    ctab, stab = jnp.tile(ctab, (1, reps)), jnp.tile(stab, (1, reps))
    kcol, vcol = ATT_WIDTH // KV_WIDTH, ATT_WIDTH // KV_WIDTH + 1
    cur = lambda b, i: b * nb + i
    prev = lambda b, i: b * nb + jnp.maximum(i - 1, 0)
    tspec_c = pl.BlockSpec((BLOCK, GROUP), lambda b, i: (i, 0))
    tspec_p = pl.BlockSpec((BLOCK, GROUP), lambda b, i: (jnp.maximum(i - 1, 0), 0))
    wspec = pl.BlockSpec((1, GROUP), lambda b, i: (0, 0))
    return pl.pallas_call(
        _attn_prompt_body,
        grid=(B, nb),
        in_specs=[pl.BlockSpec((BLOCK, ATT_WIDTH), lambda b, i: (cur(b, i), 0)),
                  pl.BlockSpec((BLOCK, KV_WIDTH), lambda b, i: (cur(b, i), kcol)),
                  pl.BlockSpec((BLOCK, KV_WIDTH), lambda b, i: (prev(b, i), kcol)),
                  pl.BlockSpec((BLOCK, KV_WIDTH), lambda b, i: (cur(b, i), vcol)),
                  pl.BlockSpec((BLOCK, KV_WIDTH), lambda b, i: (prev(b, i), vcol)),
                  tspec_c, tspec_c, tspec_p, tspec_p, wspec, wspec,
                  pl.BlockSpec(memory_space=pltpu.SMEM),
                  pl.BlockSpec((GROUP, GROUP), lambda b, i: (0, 0))],
        out_specs=[pl.BlockSpec((BLOCK, ATT_WIDTH), lambda b, i: (cur(b, i), 0)),
                   pl.BlockSpec((BLOCK, KV_WIDTH), lambda b, i: (b, 0))],
        out_shape=[jax.ShapeDtypeStruct((B * T, ATT_WIDTH), BF16),
                   jax.ShapeDtypeStruct((B * BLOCK, KV_WIDTH), F32)],
        compiler_params=_params(("arbitrary", "arbitrary")),
        name="attention_prompt",
    )(u_att, u_att, u_att, u_att, u_att, ctab, stab, ctab, stab, jnp.tile(q_norm, reps)[None, :],
      jnp.tile(k_norm, reps)[None, :], sinks, _head_sum_matrix())


def _attn_sample_body(u_ref, k_ref, v_ref, ct_ref, st_ref, qn_ref, kn_ref, sink_ref, swap_ref, rexp_ref, bmask_ref,
                      o_ref, newk_ref, newv_ref, q_s, kx_s, vx_s, kn_s):
    b = pl.program_id(0)
    rexp = rexp_ref[...]

    @pl.when(b == 0)
    def _():
        swap = swap_ref[...]
        ct, st = ct_ref[...], st_ref[...]
        for h in range(N_Q_HEADS):
            sl = slice(h * HEAD_DIM, (h + 1) * HEAD_DIM)
            q_s[:, sl] = _head_norm_rope(u_ref[:, sl], qn_ref[...], ct, st, swap)
        for g in range(N_KV_HEADS):
            sl = slice(g * HEAD_DIM, (g + 1) * HEAD_DIM)
            kn_s[:, sl] = _head_norm_rope(u_ref[:, ATT_WIDTH + g * HEAD_DIM:ATT_WIDTH + (g + 1) * HEAD_DIM],
                                          kn_ref[...], ct, st, swap)
        kx_s[...] = _dot_exact_rhs(kn_s[...], rexp)
        vx_s[...] = _dot_exact_rhs(u_ref[:, ATT_WIDTH + KV_WIDTH:], rexp)

    bmask = bmask_ref[...]
    qfull = q_s[pl.ds(b, 1), :] * bmask
    kb, vb = k_ref[...], v_ref[...]
    kexp = _dot_exact_rhs(kb, rexp)
    vexp = _dot_exact_rhs(vb, rexp)
    s = lax.dot_general(qfull, kexp, (((1,), (1,)), ((), ())), preferred_element_type=F32,
                        precision=lax.Precision.HIGHEST) * ATT_SCALE
    s_self = jnp.sum(qfull * kx_s[pl.ds(b, 1), :], axis=-1, keepdims=True) * ATT_SCALE
    sink = sink_ref[...]
    m = jnp.maximum(jnp.maximum(jnp.max(s, axis=-1, keepdims=True), s_self), sink)
    p = jnp.exp(s - m)
    p_self = jnp.exp(s_self - m)
    denom = jnp.sum(p, axis=-1, keepdims=True) + p_self + jnp.exp(sink - m)
    o = _hp_dot(p / denom, vexp)
    o = o + (p_self / denom) * vx_s[pl.ds(b, 1), :]
    o_ref[pl.ds(b, 1), :] = jnp.sum(o * bmask, axis=0, keepdims=True)
    last = lax.broadcasted_iota(I32, kb.shape, 0) == WINDOW - 1
    newk_ref[...] = jnp.where(last, kn_s[pl.ds(b, 1), :], pltpu.roll(kb, WINDOW - 1, axis=0))
    newv_ref[...] = jnp.where(last, u_ref[pl.ds(b, 1), ATT_WIDTH + KV_WIDTH:], pltpu.roll(vb, WINDOW - 1, axis=0))


def attention_sample(u_att, row0, Bs, cache_k, cache_v, layer, q_norm, k_norm, sinks):
    assert row0 % Bs == 0
    ctab, stab, swap = _rope_tables(jnp.full((1,), PAST_LEN, I32))
    gd = jnp.arange(KV_WIDTH)
    hd = jnp.arange(ATT_WIDTH)
    rexp = ((gd[:, None] // HEAD_DIM == hd[None, :] // (HEAD_DIM * Q_PER_KV))
            & (gd[:, None] % HEAD_DIM == hd[None, :] % HEAD_DIM)).astype(BF16)
    bmask = (jnp.arange(N_Q_HEADS)[:, None] == hd[None, :] // HEAD_DIM).astype(F32)
    full = lambda shape: pl.BlockSpec(shape, lambda b: (0,) * len(shape))
    cspec = pl.BlockSpec((None, None, WINDOW, KV_WIDTH), lambda b: (layer, b, 0, 0))
    ospec = pl.BlockSpec((None, WINDOW, KV_WIDTH), lambda b: (b, 0, 0))
    return pl.pallas_call(
        _attn_sample_body,
        grid=(Bs,),
        in_specs=[pl.BlockSpec((Bs, ATT_PROJ), lambda b: (row0 // Bs, 0)), cspec, cspec,
                  full((1, HEAD_DIM)), full((1, HEAD_DIM)), full((1, HEAD_DIM)), full((1, HEAD_DIM)),
                  full((N_Q_HEADS, 1)), full((HEAD_DIM, HEAD_DIM)), full((KV_WIDTH, ATT_WIDTH)),
                  full((N_Q_HEADS, ATT_WIDTH))],
        out_specs=[full((Bs, ATT_WIDTH)), ospec, ospec],
        out_shape=[jax.ShapeDtypeStruct((Bs, ATT_WIDTH), F32),
                   jax.ShapeDtypeStruct((Bs, WINDOW, KV_WIDTH), F32),
                   jax.ShapeDtypeStruct((Bs, WINDOW, KV_WIDTH), F32)],
        scratch_shapes=[pltpu.VMEM((Bs, ATT_WIDTH), F32), pltpu.VMEM((Bs, ATT_WIDTH), F32),
                        pltpu.VMEM((Bs, ATT_WIDTH), F32), pltpu.VMEM((Bs, KV_WIDTH), F32)],
        compiler_params=_params(("arbitrary",)),
        name="attention_sample",
    )(u_att, cache_k, cache_v, ctab, stab, q_norm[None, :], k_norm[None, :], sinks[:, None], swap, rexp, bmask)


def _head_sum_matrix():
    i = jnp.arange(GROUP)
    return (i[:, None] // RWKV_HEAD == i[None, :] // RWKV_HEAD).astype(BF16)


def _head_sum(x, gmat):
    return jnp.concatenate([_dot_exact_rhs(x[:, q * GROUP:(q + 1) * GROUP], gmat) for q in range(N_GROUPS)], axis=1)


def _rwkv_prep_body(*refs, shifted, has_vres):
    it = iter(refs)
    u_ref = next(it)
    prev_ref = next(it)
    mu_ref, w0_ref, a0_ref, kk_ref, ka_ref, rk_ref, lora_ref, g_ref = (next(it) for _ in range(8))
    if has_vres:
        v0_ref, v1_ref, v2_ref, vf_ref = (next(it) for _ in range(4))
    r_o, w_o, k_o, v_o, a_o, b_o, g_o, bonus_o, tail_s = it
    u = u_ref[...]
    if shifted:
        first = jnp.where(pl.program_id(1) > 0, prev_ref[SUBLANES - 1:SUBLANES, :], 0.0)
        prev = jnp.where(lax.broadcasted_iota(I32, u.shape, 0) == 0, first, pltpu.roll(u, 1, axis=0))
    else:
        prev = prev_ref[...]
    xs = u + (prev - u) * mu_ref[...]
    r = xs[:, :RWKV_WIDTH]
    k = xs[:, RWKV_WIDTH:2 * RWKV_WIDTH]
    v = xs[:, 2 * RWKV_WIDTH:3 * RWKV_WIDTH]
    tail_s[...] = jnp.zeros(tail_s.shape, F32)
    tail_s[:, :LORA_IN] = xs[:, 3 * RWKV_WIDTH:]
    t = tail_s[...]
    def dot(x, w):
        if w.dtype == F32:
            return _hp_dot(x, w)
        return jnp.dot(x.astype(BF16), w, preferred_element_type=F32)

    dw = dot(jnp.tanh(t), lora_ref[0])
    da = dot(t, lora_ref[1])
    g = dot(jax.nn.sigmoid(t), lora_ref[2])
    z = -(w0_ref[...] + dw)
    softplus = jnp.maximum(z, 0.0) + jnp.log(1.0 + jnp.exp(-jnp.abs(z)))
    w_log = -softplus - 0.5
    decay = jnp.exp(-jnp.exp(w_log))
    a = jax.nn.sigmoid(a0_ref[...] + da)
    if has_vres:
        lo = dot(dot(v, v1_ref[...]), v2_ref[...])
        v = v + (vf_ref[...] - v) * jax.nn.sigmoid(v0_ref[...] + lo)
    gmat = g_ref[...]
    kk = k * kk_ref[...]
    kk = kk / jnp.maximum(jnp.sqrt(_head_sum(kk * kk, gmat)), 1e-12)
    k = k * (1.0 + (a - 1.0) * ka_ref[...])
    r_o[...] = r
    w_o[...] = decay
    k_o[...] = k
    v_o[...] = v
    a_o[...] = -kk
    b_o[...] = kk * a
    g_o[...] = g
    bonus_o[...] = _head_sum(r * k * rk_ref[...], gmat) * v


def rwkv_prep(u_rw, row0, B, T, prev_rows, p, v_first, tr):
    n = B * T
    shifted = prev_rows is None
    if shifted:
        assert T % tr == 0 and row0 == 0
        nt = T // tr
        grid = (B, nt)
        rowblk = lambda b, i: b * nt + i
        prev_spec = pl.BlockSpec((SUBLANES, SHIFT_WIDTH),
                                 lambda b, i: (jnp.maximum(rowblk(b, i) * (tr // SUBLANES) - 1, 0), 0))
        prev_arr = u_rw
        oblk = lambda b, i: (b * nt + i, 0)
    else:
        assert T == 1 and tr == B and row0 % tr == 0
        grid = (1, 1)
        rowblk = lambda b, i: row0 // tr
        prev_spec = pl.BlockSpec((tr, SHIFT_WIDTH), lambda b, i: (0, 0))
        prev_arr = prev_rows
        oblk = lambda b, i: (0, 0)
    vec = lambda w: pl.BlockSpec((1, w), lambda b, i: (0, 0))
    in_specs = [pl.BlockSpec((tr, SHIFT_WIDTH), lambda b, i: (rowblk(b, i), 0)), prev_spec,
                vec(SHIFT_WIDTH), vec(RWKV_WIDTH), vec(RWKV_WIDTH), vec(RWKV_WIDTH), vec(RWKV_WIDTH),
                vec(RWKV_WIDTH), pl.BlockSpec((3, LORA_PAD, RWKV_WIDTH), lambda b, i: (0, 0, 0)),
                pl.BlockSpec((GROUP, GROUP), lambda b, i: (0, 0))]
    args = [u_rw, prev_arr, p['mu'], p['w0'], p['a0'], p['k_k'], p['k_a'], p['r_k'], p['lora'], _head_sum_matrix()]
    has_vres = v_first is not None
    if has_vres:
        in_specs += [vec(RWKV_WIDTH), pl.BlockSpec((RWKV_WIDTH, MV_LORA), lambda b, i: (0, 0)),
                     pl.BlockSpec((MV_LORA, RWKV_WIDTH), lambda b, i: (0, 0)),
                     pl.BlockSpec((tr, RWKV_WIDTH), oblk)]
        args += [p['v0'], p['v1'], p['v2'], v_first]
    ospec = pl.BlockSpec((tr, RWKV_WIDTH), oblk)
    return pl.pallas_call(
        functools.partial(_rwkv_prep_body, shifted=shifted, has_vres=has_vres),
        grid=grid, in_specs=in_specs, out_specs=[ospec] * 8,
        out_shape=[jax.ShapeDtypeStruct((n, RWKV_WIDTH), F32)] * 8,
        scratch_shapes=[pltpu.VMEM((tr, LORA_PAD), F32)],
        compiler_params=_params(("arbitrary", "arbitrary")),
        name="rwkv_prep",
    )(*args)


def _rwkv_params(l, shift_mu, w0, w2, a0, a2, g2, k_k, k_a, r_k, v0, v1, v2):
    lora = jnp.zeros((3, LORA_PAD, RWKV_WIDTH), F32)
    lora = lora.at[0, :DECAY_LORA].set(w2[l])
    lora = lora.at[1, DECAY_LORA:DECAY_LORA + AAA_LORA].set(a2[l])
    lora = lora.at[2, DECAY_LORA + AAA_LORA:LORA_IN].set(g2[l])
    p = {'mu': shift_mu[l][None, :], 'w0': w0[l][None, :], 'a0': a0[l][None, :], 'k_k': k_k[l][None, :],
         'k_a': k_a[l][None, :], 'r_k': r_k[l].reshape(1, RWKV_WIDTH), 'lora': lora}
    if l > 0:
        p['v0'] = v0[l - 1][None, :]
        p['v1'] = v1[l - 1]
        p['v2'] = v2[l - 1]
    p_bf16 = {k_: (v_.astype(BF16) if k_ in ('lora', 'v1', 'v2') else v_) for k_, v_ in p.items()}
    return p, p_bf16


def _wkv_body(r_ref, w_ref, k_ref, v_ref, a_ref, b_ref, s0_ref, g_ref, eye_ref, y_ref, sout_ref, s_s, sb_s, *, nb,
              tc, precise):
    c = pl.program_id(1)

    @pl.when(c == 0)
    def _():
        s_s[...] = s0_ref[...]
        sb_s[...] = s0_ref[...].astype(BF16)

    gmat = g_ref[...]
    eye = eye_ref[...][None]
    eye_b = eye.astype(BF16)
    rows = nb * RWKV_HEAD
    dot = functools.partial(jnp.dot, preferred_element_type=F32)

    groups = [slice(q * GROUP, (q + 1) * GROUP) for q in range(N_GROUPS)]

    def block(red, n):
        return red[n * rows:(n + 1) * rows].reshape(nb, RWKV_HEAD, GROUP)

    def diagonal(yx):
        d = (yx * eye).reshape(nb, RWKV_HEAD // SUBLANES, SUBLANES, GROUP)
        return jnp.sum(jnp.sum(d, axis=1), axis=1, keepdims=True)

    def store_y(t, yx):
        for q, sl in enumerate(groups):
            y_ref[:, pl.ds(t, 1), sl] = diagonal(block(yx, q))

    def step_bf16(t, carry):
        tp = jnp.maximum(t - 1, 0)
        w_t, k_t, v_t, a_t, b_t = (ref[:, pl.ds(t, 1), :] for ref in (w_ref, k_ref, v_ref, a_ref, b_ref))
        a_b, v_b, r_b = a_t.astype(BF16), v_t.astype(BF16), r_ref[:, pl.ds(tp, 1), :].astype(BF16)
        tiles = (nb, RWKV_HEAD // BF16_ROWS, BF16_ROWS, GROUP)
        lhs = []
        for q, sl in enumerate(groups):
            sb = sb_s[q]
            parts = [(sb * a_b[:, :, sl]).reshape(tiles), (eye_b * v_b[:, :, sl]).reshape(tiles),
                     (sb * r_b[:, :, sl]).reshape(tiles)]
            lhs.append(jnp.stack(parts, axis=2).reshape(3 * rows, GROUP))
        red = dot(jnp.concatenate(lhs, axis=0), gmat)
        ys = []
        for q, sl in enumerate(groups):
            rq = red[3 * q * rows:3 * (q + 1) * rows].reshape(tiles[:2] + (3,) + tiles[2:])
            sa, vx = (rq[:, :, j].reshape(nb, RWKV_HEAD, GROUP) for j in range(2))
            s_new = s_s[q] * w_t[:, :, sl] + sa * b_t[:, :, sl] + vx * k_t[:, :, sl]
            s_s[q] = s_new
            sb_s[q] = s_new.astype(BF16)
            ys.append(diagonal(rq[:, :, 2].reshape(nb, RWKV_HEAD, GROUP)))

        @pl.when(t > 0)
        def _():
            for q, sl in enumerate(groups):
                y_ref[:, pl.ds(tp, 1), sl] = ys[q]

        return carry

    def last_y_bf16():
        r_b = r_ref[:, pl.ds(tc - 1, 1), :].astype(BF16)
        lhs = [(sb_s[q] * r_b[:, :, sl]).reshape(rows, GROUP) for q, sl in enumerate(groups)]
        store_y(tc - 1, dot(jnp.concatenate(lhs, axis=0), gmat))

    def head_sums3(xs):
        lhs = []
        for x in xs:
            lhs += list(_split3(x.reshape(rows, GROUP)))
        red = dot(jnp.concatenate(lhs, axis=0), gmat)
        out = []
        for q in range(len(xs)):
            o = 3 * q * rows
            acc = red[o:o + rows] + red[o + rows:o + 2 * rows] + red[o + 2 * rows:o + 3 * rows]
            out.append(acc.reshape(nb, RWKV_HEAD, GROUP))
        return out

    def step_f32(t, carry):
        r_t, w_t, k_t, v_t, a_t, b_t = (ref[:, pl.ds(t, 1), :] for ref in (r_ref, w_ref, k_ref, v_ref, a_ref, b_ref))
        groups = [slice(q * GROUP, (q + 1) * GROUP) for q in range(N_GROUPS)]
        sa = head_sums3([s_s[q] * a_t[:, :, sl] for q, sl in enumerate(groups)])
        vx = head_sums3([eye * v_t[:, :, sl] for sl in groups])
        prods = []
        for q, sl in enumerate(groups):
            s_new = s_s[q] * w_t[:, :, sl] + sa[q] * b_t[:, :, sl] + vx[q] * k_t[:, :, sl]
            s_s[q] = s_new
            prods.append(s_new * r_t[:, :, sl])
        yx = head_sums3(prods)
        for q, sl in enumerate(groups):
            y_ref[:, pl.ds(t, 1), sl] = jnp.sum(yx[q] * eye, axis=1, keepdims=True)
        return carry

    if precise:
        lax.fori_loop(0, tc, step_f32, 0)
    else:
        lax.fori_loop(0, tc, step_bf16, 0)
        last_y_bf16()

    @pl.when(c == pl.num_programs(1) - 1)
    def _():
        sout_ref[...] = s_s[...]


def wkv_scan(r, w, k, v, a, b, s0, nb=4, tc=64, precise=False):
    B, T, W = r.shape
    tc = min(tc, T)
    assert B % nb == 0 and T % tc == 0
    xspec = pl.BlockSpec((nb, tc, W), lambda i, c: (i, c, 0))
    sspec = pl.BlockSpec((N_GROUPS, nb, RWKV_HEAD, GROUP), lambda i, c: (0, i, 0, 0))
    j = jnp.arange(GROUP)
    eye = (jnp.arange(RWKV_HEAD)[:, None] == j[None, :] % RWKV_HEAD).astype(F32)
    return pl.pallas_call(
        functools.partial(_wkv_body, nb=nb, tc=tc, precise=precise),
        grid=(B // nb, T // tc),
        in_specs=[xspec] * 6 + [sspec, pl.BlockSpec((GROUP, GROUP), lambda i, c: (0, 0)),
                                pl.BlockSpec((RWKV_HEAD, GROUP), lambda i, c: (0, 0))],
        out_specs=[xspec, sspec],
        out_shape=[jax.ShapeDtypeStruct((B, T, W), F32), jax.ShapeDtypeStruct(s0.shape, F32)],
        scratch_shapes=[pltpu.VMEM((N_GROUPS, nb, RWKV_HEAD, GROUP), F32),
                        pltpu.VMEM((N_GROUPS, nb, RWKV_HEAD, GROUP), BF16)],
        compiler_params=_params(("arbitrary", "arbitrary")),
        name="wkv_scan",
    )(r, w, k, v, a, b, s0, _head_sum_matrix(), eye)


def _state_to_groups(s):
    B = s.shape[0]
    return s.reshape(B, N_GROUPS, HEADS_PER_GROUP, RWKV_HEAD, RWKV_HEAD).transpose(1, 0, 3, 2, 4).reshape(
        N_GROUPS, B, RWKV_HEAD, GROUP)


def _state_from_groups(s):
    B = s.shape[1]
    return s.reshape(N_GROUPS, B, RWKV_HEAD, HEADS_PER_GROUP, RWKV_HEAD).transpose(1, 0, 3, 2, 4).reshape(
        B, N_RWKV_HEADS, RWKV_HEAD, RWKV_HEAD)


def _rwkv_post_body(y_ref, bonus_ref, g_ref, lw_ref, lb_ref, gm_ref, o_ref):
    gmat = gm_ref[...]
    y = y_ref[...]
    mean = _head_sum(y, gmat) * (1.0 / RWKV_HEAD)
    d = y - mean
    var = _head_sum(d * d, gmat) * (1.0 / RWKV_HEAD)
    yn = d * lax.rsqrt(var + GN_EPS) * lw_ref[...] + lb_ref[...]
    o_ref[...] = ((yn + bonus_ref[...]) * g_ref[...]).astype(o_ref.dtype)


def rwkv_post(y, bonus, g, ln_w, ln_b, out_dtype, tr=256):
    n, W = y.shape
    tr = min(tr, n)
    assert n % tr == 0
    spec = pl.BlockSpec((tr, W), lambda i: (i, 0))
    vec = pl.BlockSpec((1, W), lambda i: (0, 0))
    return pl.pallas_call(
        _rwkv_post_body, grid=(n // tr,),
        in_specs=[spec, spec, spec, vec, vec, pl.BlockSpec((GROUP, GROUP), lambda i: (0, 0))],
        out_specs=spec, out_shape=jax.ShapeDtypeStruct((n, W), out_dtype),
        compiler_params=_params(("parallel",)),
        name="rwkv_post",
    )(y, bonus, g, ln_w[None, :], ln_b[None, :], _head_sum_matrix())


def _rwkv_mixer(u_rw, B, T, prev_rows, s0_groups, p, v_first, ln_w, ln_b, tr, precise):
    r, w, k, v, a, b, g, bonus = rwkv_prep(u_rw, 0, B, T, prev_rows, p, v_first, tr)
    sh = lambda t: t.reshape(B, T, RWKV_WIDTH)
    y, s_out = wkv_scan(sh(r), sh(w), sh(k), sh(v), sh(a), sh(b), s0_groups, precise=precise)
    out = rwkv_post(y.reshape(B * T, RWKV_WIDTH), bonus, g, ln_w, ln_b, F32 if precise else BF16)
    return out, _state_from_groups(s_out), v


TM = 1024
TM_MOE = 512
TC = 256
TR_PREP = 128
TN_SMALL = 512


def kernel(x_prompt, x_sample, state_shift, state_wkv, cache_win_k, cache_win_v, attn_norm_w, w_in, q_norm_w,
           k_norm_w, attn_sinks, shift_mu, w0, w2, a0, a2, g2, k_k, k_a, r_k, ln_x_w, ln_x_b, v0, v1, v2, w_out,
           ffn_norm_w, dense_w1, dense_w3, dense_w2, router_w, moe_w1, moe_w3, moe_w2):
    Bp, Tp, D = x_prompt.shape
    Bs, Ts, _ = x_sample.shape
    assert Ts == 1 and cache_win_k.shape[2] == WINDOW and Tp % BLOCK == 0
    Mp, Ms = Bp * Tp, Bs * Ts
    assert Mp % TM == 0 and Mp % TC == 0
    xp = x_prompt.reshape(Mp, D)
    xs = x_sample.reshape(Ms, D)
    cache_k = cache_win_k.reshape(DEPTH, Bs, WINDOW, KV_WIDTH)
    cache_v = cache_win_v.reshape(DEPTH, Bs, WINDOW, KV_WIDTH)
    n_moe_tiles = (TOP_K * (Mp + Ms)) // TM_MOE + N_EXPERTS
    zero_state = jnp.zeros((N_GROUPS, Bp, RWKV_HEAD, GROUP), F32)
    vf_p = vf_s = None
    outs = {k_: [] for k_ in ('p_row', 'p_wkv', 'p_k', 'p_v', 's_row', 's_wkv', 's_k', 's_v')}
    for l in range(DEPTH):
        te, mode = _dense_tiles(Mp, TM, l)
        p_f32, p_bf16 = _rwkv_params(l, shift_mu, w0, w2, a0, a2, g2, k_k, k_a, r_k, v0, v1, v2)
        an = attn_norm_w[l][None, :]
        h = rmsnorm_bf16(xp, an)
        u_att = matmul_proj(h, w_in, te, mode, n_lo=0, n_cols=ATT_PROJ, tm=TM, tn=512)
        u_rw = matmul_proj(h, w_in, te, mode, n_lo=ATT_PROJ, n_cols=SHIFT_WIDTH, tm=TM, tn=512)
        att_p, knew_p = attention_prompt(u_att, Bp, Tp, q_norm_w[l], k_norm_w[l], attn_sinks[l])
        rw_p, S_p, v_p_first = _rwkv_mixer(u_rw, Bp, Tp, None, zero_state, p_bf16, vf_p, ln_x_w[l], ln_x_b[l],
                                           TR_PREP, False)
        xp = matmul_residual(jnp.concatenate([att_p, rw_p], axis=1), w_out, xp, te, mode, tm=TM, tn=512, tk=D,
                             in_place=l > 0)
        us_att = small_matmul('proj', xs, [w_in], l, tn=TN_SMALL, norm_w=an, n_lo=0, n_cols=ATT_PROJ)
        us_rw = small_matmul('proj', xs, [w_in], l, tn=TN_SMALL, norm_w=an, n_lo=ATT_PROJ, n_cols=SHIFT_WIDTH)
        att_s, k_s, v_s = attention_sample(us_att, 0, Bs, cache_k, cache_v, l, q_norm_w[l], k_norm_w[l],
                                           attn_sinks[l])
        rw_s, S_s, v_s_first = _rwkv_mixer(us_rw, Bs, Ts, state_shift[l], _state_to_groups(state_wkv[l]), p_f32,
                                           vf_s, ln_x_w[l], ln_x_b[l], Bs, True)
        xs = small_matmul('res', jnp.concatenate([att_s, rw_s], axis=1), [w_out], l, tn=TN_SMALL, res=xs)
        if l == 0:
            vf_p, vf_s = v_p_first, v_s_first
        fn = ffn_norm_w[l][None, :]
        if l % 2 == 0:
            te, mode = _dense_tiles(Mp, TM, l // 2)
            h = rmsnorm_bf16(xp, fn)
            g = matmul_swiglu_in(h, dense_w1, dense_w3, te, mode, tm=TM, tn=512)
            xp = matmul_residual(g, dense_w2, xp, te, mode, tm=TM, tn=1024, tk=D_FF // 4)
            gs = small_matmul('swiglu', xs, [dense_w1, dense_w3], l // 2, tn=TN_SMALL // 2, norm_w=fn)
            xs = small_matmul('res', gs, [dense_w2], l // 2, tn=TN_SMALL // 4, res=xs)
        else:
            router_pad = jnp.pad(router_w[l // 2], ((0, 0), (0, LANES - N_EXPERTS)))
            idx_p, gates_p = route_top2(xp, fn, router_pad)
            idx_s, gates_s = route_top2(xs, fn, router_pad, tr=Ms)
            te, mode, src, gate_rows, slot = _route_tables(
                jnp.concatenate([idx_p, idx_s], axis=0), jnp.concatenate([gates_p, gates_s], axis=0),
                TM_MOE, n_moe_tiles)
            rows = gather_rmsnorm_bf16(xp, xs, fn, src, tg=TC)
            g = matmul_swiglu_in(rows, moe_w1[l // 2], moe_w3[l // 2], te, mode, tm=TM_MOE, tn=512, parts=True)
            ys = matmul_residual(g, moe_w2[l // 2], None, te, mode,
                                 tm=TM_MOE, tn=1024, tk=D_FF // 4, parts=True, scale=gate_rows)
            xp = combine_top2(xp, ys, slot[:Mp, 0], slot[:Mp, 1], tc=TC)
            xs = combine_top2(xs, ys, slot[Mp:, 0], slot[Mp:, 1], tc=Ms)
        u_att_p = u_att.reshape(Bp, Tp, ATT_PROJ)
        outs['p_row'].append(u_rw.reshape(Bp, Tp, SHIFT_WIDTH)[:, Tp - 1])
        outs['p_wkv'].append(S_p)
        outs['p_k'].append(knew_p.reshape(Bp, WINDOW, N_KV_HEADS, HEAD_DIM))
        outs['p_v'].append(u_att_p[:, Tp - WINDOW:, ATT_WIDTH + KV_WIDTH:].reshape(Bp, WINDOW, N_KV_HEADS, HEAD_DIM))
        outs['s_row'].append(us_rw)
        outs['s_wkv'].append(S_s)
        outs['s_k'].append(k_s.reshape(Bs, WINDOW, N_KV_HEADS, HEAD_DIM))
        outs['s_v'].append(v_s.reshape(Bs, WINDOW, N_KV_HEADS, HEAD_DIM))
    return (xp.reshape(Bp, Tp, D), xs.reshape(Bs, Ts, D),
            jnp.stack(outs['p_row']), jnp.stack(outs['p_wkv']), jnp.stack(outs['p_k']), jnp.stack(outs['p_v']),
            jnp.stack(outs['s_row']), jnp.stack(outs['s_wkv']), jnp.stack(outs['s_k']), jnp.stack(outs['s_v']))
```

```python
import functools
import math

import jax
import jax.numpy as jnp
from jax import lax
from jax.experimental import pallas as pl
from jax.experimental.pallas import tpu as pltpu

F32 = jnp.float32
BF16 = jnp.bfloat16
I32 = jnp.int32

D_MODEL = 4096
DEPTH = 2
PAST_LEN = 16384
HEAD_DIM = 64
ATT_WIDTH = D_MODEL // 2
N_Q_HEADS = ATT_WIDTH // HEAD_DIM
N_KV_HEADS = 8
Q_PER_KV = N_Q_HEADS // N_KV_HEADS
KV_WIDTH = N_KV_HEADS * HEAD_DIM
ATT_PROJ = ATT_WIDTH + 2 * KV_WIDTH
WINDOW = 128
BLOCK = 128
ROPE_DIM = HEAD_DIM // 4
ROPE_THETA = 500000.0
ATT_SCALE = HEAD_DIM ** -0.5
RWKV_WIDTH = D_MODEL - ATT_WIDTH
RWKV_HEAD = 64
N_RWKV_HEADS = RWKV_WIDTH // RWKV_HEAD
DECAY_LORA = 96
AAA_LORA = 96
GATE_LORA = 256
MV_LORA = 64
LORA_IN = DECAY_LORA + AAA_LORA + GATE_LORA
SHIFT_WIDTH = 3 * RWKV_WIDTH + LORA_IN
D_FF = 14336
N_EXPERTS = 8
TOP_K = 2
RMS_EPS = 1e-5
GN_EPS = 64e-5

VMEM_LIMIT_BYTES = 58 * 2 ** 20
LANES = 128
SUBLANES = 8
BF16_ROWS = 2 * SUBLANES
LORA_PAD = 4 * LANES
GROUP = 2 * LANES
N_GROUPS = RWKV_WIDTH // GROUP
HEADS_PER_GROUP = GROUP // RWKV_HEAD

MODE_SKIP, MODE_FULL = 0, 1
MODE_PART = 8
PART_ROWS = LANES
DMA_UNROLL = 8


def _cdiv(a, b):
    return -(-a // b)


def _params(sem):
    return pltpu.CompilerParams(dimension_semantics=sem, vmem_limit_bytes=VMEM_LIMIT_BYTES)


def _split3(x):
    hi = x.astype(BF16)
    r1 = x - hi.astype(F32)
    mid = r1.astype(BF16)
    lo = (r1 - mid.astype(F32)).astype(BF16)
    return hi, mid, lo


def _dot_exact_rhs(x, rhs_bf16):
    hi, mid, lo = _split3(x)
    d = functools.partial(jnp.dot, preferred_element_type=F32)
    return d(hi, rhs_bf16) + d(mid, rhs_bf16) + d(lo, rhs_bf16)


def _rmsnorm_body(x_ref, w_ref, o_ref):
    x = x_ref[...]
    ms = jnp.mean(x * x, axis=-1, keepdims=True)
    o_ref[...] = (x * lax.rsqrt(ms + RMS_EPS) * w_ref[...]).astype(o_ref.dtype)


def rmsnorm_bf16(x, w, tr=512):
    M, D = x.shape
    return pl.pallas_call(
        _rmsnorm_body,
        grid=(_cdiv(M, tr),),
        in_specs=[pl.BlockSpec((tr, D), lambda i: (i, 0)), pl.BlockSpec((1, D), lambda i: (0, 0))],
        out_specs=pl.BlockSpec((tr, D), lambda i: (i, 0)),
        out_shape=jax.ShapeDtypeStruct((M, D), BF16),
        compiler_params=_params(("parallel",)),
        name="rmsnorm_bf16",
    )(x, w)


def _router_body(x_ref, w_ref, r_ref, idx_ref, gate_ref):
    x = x_ref[...]
    ms = jnp.mean(x * x, axis=-1, keepdims=True)
    h = x * lax.rsqrt(ms + RMS_EPS) * w_ref[...]
    logits = _dot3(h, r_ref[...])
    lane = lax.broadcasted_iota(I32, logits.shape, 1)
    neg = jnp.float32(-jnp.inf)
    l1 = jnp.where(lane < N_EXPERTS, logits, neg)
    m1 = jnp.max(l1, axis=-1, keepdims=True)
    i1 = jnp.min(jnp.where(l1 == m1, lane, LANES), axis=-1, keepdims=True)
    l2 = jnp.where(lane == i1, neg, l1)
    m2 = jnp.max(l2, axis=-1, keepdims=True)
    i2 = jnp.min(jnp.where(l2 == m2, lane, LANES), axis=-1, keepdims=True)
    e2 = jnp.exp(m2 - m1)
    den = 1.0 + e2
    idx_ref[...] = jnp.where(lane == 0, i1, jnp.where(lane == 1, i2, 0))
    gate_ref[...] = jnp.where(lane == 0, 1.0 / den, jnp.where(lane == 1, e2 / den, 0.0))


def route_top2(x, w, router_pad, tr=512):
    M, D = x.shape
    idx, gate = pl.pallas_call(
        _router_body,
        grid=(_cdiv(M, tr),),
        in_specs=[pl.BlockSpec((tr, D), lambda i: (i, 0)), pl.BlockSpec((1, D), lambda i: (0, 0)),
                  pl.BlockSpec((D, LANES), lambda i: (0, 0))],
        out_specs=[pl.BlockSpec((tr, LANES), lambda i: (i, 0)), pl.BlockSpec((tr, LANES), lambda i: (i, 0))],
        out_shape=[jax.ShapeDtypeStruct((M, LANES), I32), jax.ShapeDtypeStruct((M, LANES), F32)],
        compiler_params=_params(("parallel",)),
        name="route_top2",
    )(x, w, router_pad)
    return idx[:, :TOP_K], gate[:, :TOP_K]


def _first_of_weight(te_ref, m):
    prev = te_ref[jnp.maximum(m - 1, 0)]
    return jnp.logical_or(m == 0, te_ref[m] != prev)


def _for_rows(mode_ref, m, parts, fn, o_ref):
    mode = mode_ref[m]
    pl.when(mode == MODE_FULL)(lambda: fn(slice(None)))

    def zero(sl):
        o_ref[sl, :] = jnp.zeros(o_ref[sl, :].shape, o_ref.dtype)

    pl.when(mode == MODE_SKIP)(lambda: zero(slice(None)))
    if parts:
        tm = o_ref.shape[0]
        assert tm % PART_ROWS == 0
        for c in range(tm // PART_ROWS):
            sl = slice(c * PART_ROWS, (c + 1) * PART_ROWS)
            pl.when(jnp.logical_and(mode >= MODE_PART, c < mode - MODE_PART))(lambda sl=sl: fn(sl))
            pl.when(jnp.logical_and(mode >= MODE_PART, c >= mode - MODE_PART))(lambda sl=sl: zero(sl))


def _proj_body(te_ref, mode_ref, a_ref, w_ref, o_ref, wb_ref, *, parts):
    m = pl.program_id(1)
    pl.when(_first_of_weight(te_ref, m))(lambda: wb_ref.__setitem__(Ellipsis, w_ref[...].astype(BF16)))

    def rows(sl):
        o_ref[sl, :] = jnp.dot(a_ref[sl, :], wb_ref[...], preferred_element_type=F32)

    _for_rows(mode_ref, m, parts, rows, o_ref)


def matmul_proj(a, w, te, mode, *, n_lo, n_cols, tm, tn, parts=False):
    M, K = a.shape
    nb0 = n_lo // tn
    assert nb0 * tn == n_lo
    grid = (_cdiv(n_cols, tn), _cdiv(M, tm))
    return pl.pallas_call(
        functools.partial(_proj_body, parts=parts),
        grid_spec=pltpu.PrefetchScalarGridSpec(
            num_scalar_prefetch=2, grid=grid,
            in_specs=[pl.BlockSpec((tm, K), lambda n, m, te, md: (m, 0)),
                      pl.BlockSpec((None, K, tn), lambda n, m, te, md: (te[m], 0, nb0 + n))],
            out_specs=pl.BlockSpec((tm, tn), lambda n, m, te, md: (m, n)),
            scratch_shapes=[pltpu.VMEM((K, tn), BF16)]),
        out_shape=jax.ShapeDtypeStruct((M, n_cols), F32),
        compiler_params=_params(("arbitrary", "arbitrary")),
        name="matmul_proj",
    )(te, mode, a, w)


def _next_weight(te):
    big = jnp.int32(2 ** 30)
    nxt = jnp.min(jnp.where(te[None, :] > te[:, None], te[None, :], big), axis=1)
    return jnp.where(nxt == big, -1, nxt).astype(I32)


def _stream_weights(te_ref, nxt_ref, w_hbms, wf_refs, wb_refs, sems, *, k_lo, tk, tn):
    n, m = pl.program_id(0), pl.program_id(1)

    def copy(i, e, nn):
        cols = pl.ds(pl.multiple_of(nn * tn, tn), tn)
        return pltpu.make_async_copy(w_hbms[i].at[e, pl.ds(k_lo, tk), cols], wf_refs[i], sems.at[i])

    @pl.when(_first_of_weight(te_ref, m))
    def _():
        e = te_ref[m]

        @pl.when(jnp.logical_and(n == 0, m == 0))
        def _():
            for i in range(len(w_hbms)):
                copy(i, e, n).start()

        for i in range(len(w_hbms)):
            copy(i, e, n).wait()
            wb_refs[i][...] = wf_refs[i][...].astype(BF16)
        nxt = nxt_ref[m]
        same_sweep = nxt >= 0
        e2 = jnp.where(same_sweep, nxt, te_ref[0])
        n2 = jnp.where(same_sweep, n, n + 1)

        @pl.when(jnp.logical_or(same_sweep, n + 1 < pl.num_programs(0)))
        def _():
            for i in range(len(w_hbms)):
                copy(i, e2, n2).start()


def _ffn1_body(te_ref, mode_ref, nxt_ref, a_ref, w1_hbm, w3_hbm, o_ref, wf1_ref, wf3_ref, wb1_ref, wb3_ref, sems,
               *, parts, tn):
    m = pl.program_id(1)
    _stream_weights(te_ref, nxt_ref, (w1_hbm, w3_hbm), (wf1_ref, wf3_ref), (wb1_ref, wb3_ref), sems,
                    k_lo=0, tk=wf1_ref.shape[0], tn=tn)

    def rows(sl):
        a = a_ref[sl, :]
        u1 = jnp.dot(a, wb1_ref[...], preferred_element_type=F32)
        u3 = jnp.dot(a, wb3_ref[...], preferred_element_type=F32)
        o_ref[sl, :] = (u1 * jax.nn.sigmoid(u1) * u3).astype(o_ref.dtype)

    _for_rows(mode_ref, m, parts, rows, o_ref)


def matmul_swiglu_in(a, w1, w3, te, mode, *, tm, tn, parts=False):
    M, K = a.shape
    F = w1.shape[-1]
    assert F % tn == 0
    grid = (F // tn, _cdiv(M, tm))
    wspec = pl.BlockSpec(memory_space=pl.ANY)
    return pl.pallas_call(
        functools.partial(_ffn1_body, parts=parts, tn=tn),
        grid_spec=pltpu.PrefetchScalarGridSpec(
            num_scalar_prefetch=3, grid=grid,
            in_specs=[pl.BlockSpec((tm, K), lambda n, m, te, md, nx: (m, 0)), wspec, wspec],
            out_specs=pl.BlockSpec((tm, tn), lambda n, m, te, md, nx: (m, n)),
            scratch_shapes=[pltpu.VMEM((K, tn), F32), pltpu.VMEM((K, tn), F32),
                            pltpu.VMEM((K, tn), BF16), pltpu.VMEM((K, tn), BF16),
                            pltpu.SemaphoreType.DMA((2,))]),
        out_shape=jax.ShapeDtypeStruct((M, F), BF16),
        compiler_params=_params(("arbitrary", "arbitrary")),
        name="matmul_swiglu_in",
    )(te, mode, _next_weight(te), a, w1, w3)


def _res_body(te_ref, mode_ref, nxt_ref, a_ref, w_hbm, *rest, parts, has_res, scaled, k_lo, tn):
    rest = list(rest)
    res_ref = rest.pop(0) if has_res else None
    scale_ref = rest.pop(0) if scaled else None
    o_ref, wf_ref, wb_ref, sems = rest
    m = pl.program_id(1)
    _stream_weights(te_ref, nxt_ref, (w_hbm,), (wf_ref,), (wb_ref,), sems, k_lo=k_lo, tk=wf_ref.shape[0], tn=tn)

    def rows(sl):
        d = jnp.dot(a_ref[sl, :], wb_ref[...], preferred_element_type=F32)
        if scaled:
            d = d * scale_ref[sl, :]
        o_ref[sl, :] = res_ref[sl, :] + d if has_res else d

    _for_rows(mode_ref, m, parts, rows, o_ref)


def matmul_residual(a, w, res, te, mode, *, tm, tn, tk, parts=False, scale=None, in_place=True):
    M, K = a.shape
    N = w.shape[-1]
    assert K % tk == 0 and N % tn == 0
    grid = (N // tn, _cdiv(M, tm))
    nxt = _next_weight(te)
    for kb in range(K // tk):
        in_specs = [pl.BlockSpec((tm, tk), lambda n, m, te, md, nx, kb=kb: (m, kb)),
                    pl.BlockSpec(memory_space=pl.ANY)]
        args = [te, mode, nxt, a, w]
        aliases = {}
        if res is not None:
            if in_place or kb > 0:
                aliases = {len(args): 0}
            in_specs.append(pl.BlockSpec((tm, tn), lambda n, m, te, md, nx: (m, n)))
            args.append(res)
        if scale is not None:
            in_specs.append(pl.BlockSpec((tm, 1), lambda n, m, te, md, nx: (m, 0)))
            args.append(scale)
        res = pl.pallas_call(
            functools.partial(_res_body, parts=parts, has_res=res is not None, scaled=scale is not None,
                              k_lo=kb * tk, tn=tn),
            grid_spec=pltpu.PrefetchScalarGridSpec(
                num_scalar_prefetch=3, grid=grid, in_specs=in_specs,
                out_specs=pl.BlockSpec((tm, tn), lambda n, m, te, md, nx: (m, n)),
                scratch_shapes=[pltpu.VMEM((tk, tn), F32), pltpu.VMEM((tk, tn), BF16),
                                pltpu.SemaphoreType.DMA((1,))]),
            out_shape=jax.ShapeDtypeStruct((M, N), F32),
            input_output_aliases=aliases,
            compiler_params=_params(("arbitrary", "arbitrary")),
            name="matmul_residual",
        )(*args)
    return res


def _dense_tiles(M, tm, e):
    assert M % tm == 0
    return jnp.full((M // tm,), e, I32), jnp.full((M // tm,), MODE_FULL, I32)


_hp_dot = functools.partial(jnp.dot, preferred_element_type=F32, precision=lax.Precision.HIGHEST)


def _split2(x):
    hi = x.astype(BF16)
    return hi, (x - hi.astype(F32)).astype(BF16)


def _dot3(a, w):
    a_hi, a_lo = _split2(a)
    w_hi, w_lo = _split2(w)
    r = a.shape[0]
    first = jnp.dot(jnp.concatenate([a_hi, a_lo], axis=0), w_hi, preferred_element_type=F32)
    return first[:r] + first[r:] + jnp.dot(a_hi, w_lo, preferred_element_type=F32)


def _small_body(*refs, kind, normed):
    it = iter(refs)
    x_ref = next(it)
    nw_ref = next(it) if normed else None
    a = x_ref[...]
    if normed:
        a = a * lax.rsqrt(jnp.mean(a * a, axis=-1, keepdims=True) + RMS_EPS) * nw_ref[...]
    if kind == 'swiglu':
        w1_ref, w3_ref, o_ref = it
        u1 = _dot3(a, w1_ref[...])
        u3 = _dot3(a, w3_ref[...])
        o_ref[...] = u1 * jax.nn.sigmoid(u1) * u3
    elif kind == 'res':
        w_ref, res_ref, o_ref = it
        o_ref[...] = res_ref[...] + _dot3(a, w_ref[...])
    else:
        w_ref, o_ref = it
        o_ref[...] = _dot3(a, w_ref[...])


def small_matmul(kind, x, ws, e, *, tn, norm_w=None, res=None, n_lo=0, n_cols=None):
    R, K = x.shape
    N = ws[0].shape[-1]
    n_cols = N - n_lo if n_cols is None else n_cols
    nb0 = n_lo // tn
    assert nb0 * tn == n_lo
    normed = norm_w is not None
    row = lambda w: pl.BlockSpec((R, w), lambda n: (0, 0))
    in_specs, args = [row(K)], [x]
    if normed:
        in_specs.append(pl.BlockSpec((1, K), lambda n: (0, 0)))
        args.append(norm_w)
    for w in ws:
        in_specs.append(pl.BlockSpec((None, K, tn), lambda n: (e, 0, nb0 + n)))
        args.append(w)
    ospec = pl.BlockSpec((R, tn), lambda n: (0, n))
    aliases = {}
    if kind == 'res':
        in_specs.append(ospec)
        aliases = {len(args): 0}
        args.append(res)
    return pl.pallas_call(
        functools.partial(_small_body, kind=kind, normed=normed),
        grid=(_cdiv(n_cols, tn),), in_specs=in_specs, out_specs=ospec,
        out_shape=jax.ShapeDtypeStruct((R, n_cols), F32),
        input_output_aliases=aliases,
        compiler_params=_params(("arbitrary",)),
        name="small_matmul_" + kind,
    )(*args)


def _gather_norm_body(src_ref, xp_hbm, xs_hbm, w_ref, o_ref, buf, sem, *, tg, mp):
    def issue(j, c):
        for p in range(2):
            r = 2 * j + p
            t = src_ref[r]
            dst = buf.at[pl.ds(r, 1), :]
            pl.when(t < mp)(
                lambda t=t, dst=dst, p=p: pltpu.make_async_copy(xp_hbm.at[pl.ds(t, 1), :], dst, sem).start(priority=p))
            pl.when(t >= mp)(
                lambda t=t, dst=dst, p=p: pltpu.make_async_copy(xs_hbm.at[pl.ds(t - mp, 1), :], dst, sem).start(
                    priority=p))
        return c

    assert tg % (2 * DMA_UNROLL) == 0
    lax.fori_loop(0, tg // 2, issue, 0, unroll=DMA_UNROLL // 2)

    def wait(r, c):
        pltpu.make_async_copy(xp_hbm.at[pl.ds(0, 1), :], buf.at[pl.ds(r, 1), :], sem).wait()
        return c

    lax.fori_loop(0, tg, wait, 0, unroll=DMA_UNROLL)
    x = buf[...]
    ms = jnp.mean(x * x, axis=-1, keepdims=True)
    o_ref[...] = (x * lax.rsqrt(ms + RMS_EPS) * w_ref[...]).astype(o_ref.dtype)


def gather_rmsnorm_bf16(xp, xs, w, src, tg=256):
    Mp, D = xp.shape
    R = src.shape[0]
    assert R % tg == 0
    return pl.pallas_call(
        functools.partial(_gather_norm_body, tg=tg, mp=Mp),
        grid=(R // tg,),
        in_specs=[pl.BlockSpec((tg,), lambda i: (i,), memory_space=pltpu.SMEM),
                  pl.BlockSpec(memory_space=pl.ANY), pl.BlockSpec(memory_space=pl.ANY),
                  pl.BlockSpec((1, D), lambda i: (0, 0))],
        out_specs=pl.BlockSpec((tg, D), lambda i: (i, 0)),
        out_shape=jax.ShapeDtypeStruct((R, D), BF16),
        scratch_shapes=[pltpu.VMEM((tg, D), F32), pltpu.SemaphoreType.DMA(())],
        compiler_params=_params(("arbitrary",)),
        name="gather_rmsnorm_bf16",
    )(src, xp, xs, w)


def _combine_body(s0_ref, s1_ref, x_ref, y_hbm, o_ref, buf0, buf1, sem, *, tc):
    def issue(r, c):
        pltpu.make_async_copy(y_hbm.at[pl.ds(s0_ref[r], 1), :], buf0.at[pl.ds(r, 1), :], sem).start(priority=0)
        pltpu.make_async_copy(y_hbm.at[pl.ds(s1_ref[r], 1), :], buf1.at[pl.ds(r, 1), :], sem).start(priority=1)
        return c

    lax.fori_loop(0, tc, issue, 0, unroll=DMA_UNROLL)

    def wait(r, c):
        pltpu.make_async_copy(y_hbm.at[pl.ds(0, 1), :], buf0.at[pl.ds(r, 1), :], sem).wait()
        pltpu.make_async_copy(y_hbm.at[pl.ds(0, 1), :], buf1.at[pl.ds(r, 1), :], sem).wait()
        return c

    lax.fori_loop(0, tc, wait, 0, unroll=DMA_UNROLL)
    o_ref[...] = x_ref[...] + (buf0[...] + buf1[...])


def combine_top2(x, ys, slot0, slot1, tc=256):
    M, D = x.shape
    nt = _cdiv(M, tc)
    assert slot0.shape[0] == nt * tc
    sspec = pl.BlockSpec((tc,), lambda i: (i,), memory_space=pltpu.SMEM)
    return pl.pallas_call(
        functools.partial(_combine_body, tc=tc),
        grid=(nt,),
        in_specs=[sspec, sspec, pl.BlockSpec((tc, D), lambda i: (i, 0)), pl.BlockSpec(memory_space=pl.ANY)],
        out_specs=pl.BlockSpec((tc, D), lambda i: (i, 0)),
        out_shape=jax.ShapeDtypeStruct((M, D), F32),
        scratch_shapes=[pltpu.VMEM((tc, D), F32), pltpu.VMEM((tc, D), F32), pltpu.SemaphoreType.DMA(())],
        compiler_params=_params(("arbitrary",)),
        name="combine_top2",
    )(slot0, slot1, x, ys)


def _route_tables(idx, gates, tmg, n_tiles):
    M = idx.shape[0]
    e = idx.reshape(-1)
    onehot = (e[:, None] == jnp.arange(N_EXPERTS, dtype=I32)[None, :]).astype(I32)
    csum = jnp.cumsum(onehot, axis=0)
    rank = jnp.take_along_axis(csum, e[:, None], axis=1)[:, 0] - 1
    counts = csum[-1]
    tiles_e = (counts + tmg - 1) // tmg
    tile_end = jnp.cumsum(tiles_e)
    tile_start = tile_end - tiles_e
    slot = tile_start[e] * tmg + rank
    tiles = jnp.arange(n_tiles, dtype=I32)
    te_raw = jnp.sum((tiles[:, None] >= tile_end[None, :]).astype(I32), axis=1)
    total = tile_end[-1]
    valid = tiles < total
    te_last = jnp.sum((total - 1 >= tile_end).astype(I32))
    te = jnp.where(valid, te_raw, te_last).astype(I32)
    tile_rows = jnp.clip(counts[te] - (tiles - tile_start[te]) * tmg, 0, tmg)
    parts = (tile_rows + PART_ROWS - 1) // PART_ROWS
    mode = jnp.where(valid, jnp.where(parts * PART_ROWS >= tmg, MODE_FULL, MODE_PART + parts), MODE_SKIP)
    mode = mode.astype(I32)
    rows = n_tiles * tmg
    src = jnp.zeros((rows,), I32).at[slot].set(jnp.arange(2 * M, dtype=I32) // 2)
    gate_s = jnp.zeros((rows,), F32).at[slot].set(gates.reshape(-1))
    return te, mode, src, gate_s[:, None], slot.reshape(M, TOP_K)


def _rope_tables(pos):
    half = ROPE_DIM // 2
    inv_freq = jnp.exp(-math.log(ROPE_THETA) * 2.0 * jnp.arange(half, dtype=F32) / ROPE_DIM)
    ang = pos.astype(F32)[:, None] * inv_freq[None, :]
    cos, sin = jnp.cos(ang), jnp.sin(ang)
    T = pos.shape[0]
    rest = HEAD_DIM - ROPE_DIM
    ctab = jnp.concatenate([cos, cos, jnp.ones((T, rest), F32)], axis=1)
    stab = jnp.concatenate([-sin, sin, jnp.zeros((T, rest), F32)], axis=1)
    d = jnp.arange(HEAD_DIM)
    partner = jnp.where(d < half, d + half, jnp.where(d < ROPE_DIM, d - half, d))
    swap = (d[:, None] == partner[None, :]).astype(BF16)
    return ctab, stab, swap


def _head_norm_rope(x, nw, ctab, stab, swap):
    y = x * lax.rsqrt(jnp.mean(x * x, axis=-1, keepdims=True) + RMS_EPS) * nw
    return y * ctab + _dot_exact_rhs(y, swap) * stab


def _norm_rope_wide(x, nw, ctab, stab, gmat):
    outs = []
    lane = lax.broadcasted_iota(I32, (x.shape[0], GROUP), 1) % HEAD_DIM
    half = ROPE_DIM // 2
    for q in range(x.shape[1] // GROUP):
        xg = x[:, q * GROUP:(q + 1) * GROUP]
        sq = xg * xg
        hi = sq.astype(BF16)
        lo = (sq - hi.astype(F32)).astype(BF16)
        ss = jnp.dot(hi, gmat, preferred_element_type=F32) + jnp.dot(lo, gmat, preferred_element_type=F32)
        y = xg * lax.rsqrt(ss * (1.0 / HEAD_DIM) + RMS_EPS) * nw
        partner = jnp.where(lane < half, pltpu.roll(y, GROUP - half, axis=1), pltpu.roll(y, half, axis=1))
        outs.append(y * ctab + partner * stab)
    return outs


def _attn_prompt_body(q_ref, kc_ref, kp_ref, vc_ref, vp_ref, cc_ref, sc_ref, cp_ref, sp_ref, qn_ref, kn_ref,
                      sink_ref, g_ref, o_ref, knew_ref):
    i = pl.program_id(1)
    gmat = g_ref[...]
    cc, sc, cp, sp = cc_ref[...], sc_ref[...], cp_ref[...], sp_ref[...]
    qn, kn = qn_ref[...], kn_ref[...]
    R = Q_PER_KV * BLOCK
    row = lax.broadcasted_iota(I32, (R, 2 * BLOCK), 0)
    col = lax.broadcasted_iota(I32, (R, 2 * BLOCK), 1)
    rel = (row % BLOCK) - col + BLOCK
    mask = (rel >= 0) & (rel <= WINDOW) & jnp.logical_not((i == 0) & (col < BLOCK))
    rowh = lax.broadcasted_iota(I32, (R, 1), 0) // BLOCK
    headmask = (lax.broadcasted_iota(I32, (R, GROUP), 0) // BLOCK
                == lax.broadcasted_iota(I32, (R, GROUP), 1) // HEAD_DIM)
    lane_lo = lax.broadcasted_iota(I32, (2 * BLOCK, LANES), 1) < HEAD_DIM
    heads_per_group = GROUP // HEAD_DIM

    q_rot = _norm_rope_wide(q_ref[...], qn, cc, sc, gmat)
    k_cur = _norm_rope_wide(kc_ref[...], kn, cc, sc, gmat)
    k_prev = _norm_rope_wide(kp_ref[...], kn, cp, sp, gmat)
    for j in range(KV_WIDTH // GROUP):
        knew_ref[:, j * GROUP:(j + 1) * GROUP] = k_cur[j]

    def spread(x, g):
        other = pltpu.roll(x, HEAD_DIM, axis=1)
        one = jnp.where(lane_lo, x, other) if g % 2 == 0 else jnp.where(lane_lo, other, x)
        return jnp.concatenate([one, one], axis=1)

    for g in range(N_KV_HEADS):
        j, m = g // heads_per_group, (g % heads_per_group) // 2
        lanes = slice(m * LANES, (m + 1) * LANES)
        kcat = jnp.concatenate([k_prev[j][:, lanes], k_cur[j][:, lanes]], axis=0)
        c0 = g // 2 * LANES
        vcat = jnp.concatenate([vp_ref[:, c0:c0 + LANES], vc_ref[:, c0:c0 + LANES]], axis=0)
        kx = spread(kcat, g).astype(BF16)
        vx = spread(vcat, g).astype(BF16)
        qbd = jnp.where(headmask, jnp.concatenate([q_rot[g]] * Q_PER_KV, axis=0), 0.0).astype(BF16)
        s = lax.dot_general(qbd, kx, (((1,), (1,)), ((), ())), preferred_element_type=F32) * ATT_SCALE
        s = jnp.where(mask, s, -jnp.inf)
        sink = jnp.zeros((R, 1), F32)
        for h in range(Q_PER_KV):
            sink = jnp.where(rowh == h, sink_ref[g * Q_PER_KV + h], sink)
        mx = jnp.maximum(jnp.max(s, axis=-1, keepdims=True), sink)
        p = jnp.exp(s - mx)
        denom = jnp.sum(p, axis=-1, keepdims=True) + jnp.exp(sink - mx)
        o = jnp.dot((p / denom).astype(BF16), vx, preferred_element_type=F32)
        o = jnp.where(headmask, o, 0.0)
        out = o[0:BLOCK]
        for h in range(1, Q_PER_KV):
            out = out + o[h * BLOCK:(h + 1) * BLOCK]
        o_ref[:, g * GROUP:(g + 1) * GROUP] = out.astype(o_ref.dtype)


def attention_prompt(u_att, B, T, q_norm, k_norm, sinks):
    nb = T // BLOCK
    reps = GROUP // HEAD_DIM
    ctab, stab, _ = _rope_tables(jnp.arange(T, dtype=I32))
    ctab, stab = jnp.tile(ctab, (1, reps)), jnp.tile(stab, (1, reps))
    kcol, vcol = ATT_WIDTH // KV_WIDTH, ATT_WIDTH // KV_WIDTH + 1
    cur = lambda b, i: b * nb + i
    prev = lambda b, i: b * nb + jnp.maximum(i - 1, 0)
    tspec_c = pl.BlockSpec((BLOCK, GROUP), lambda b, i: (i, 0))
    tspec_p = pl.BlockSpec((BLOCK, GROUP), lambda b, i: (jnp.maximum(i - 1, 0), 0))
    wspec = pl.BlockSpec((1, GROUP), lambda b, i: (0, 0))
    return pl.pallas_call(
        _attn_prompt_body,
        grid=(B, nb),
        in_specs=[pl.BlockSpec((BLOCK, ATT_WIDTH), lambda b, i: (cur(b, i), 0)),
                  pl.BlockSpec((BLOCK, KV_WIDTH), lambda b, i: (cur(b, i), kcol)),
                  pl.BlockSpec((BLOCK, KV_WIDTH), lambda b, i: (prev(b, i), kcol)),
                  pl.BlockSpec((BLOCK, KV_WIDTH), lambda b, i: (cur(b, i), vcol)),
                  pl.BlockSpec((BLOCK, KV_WIDTH), lambda b, i: (prev(b, i), vcol)),
                  tspec_c, tspec_c, tspec_p, tspec_p, wspec, wspec,
                  pl.BlockSpec(memory_space=pltpu.SMEM),
                  pl.BlockSpec((GROUP, GROUP), lambda b, i: (0, 0))],
        out_specs=[pl.BlockSpec((BLOCK, ATT_WIDTH), lambda b, i: (cur(b, i), 0)),
                   pl.BlockSpec((BLOCK, KV_WIDTH), lambda b, i: (b, 0))],
        out_shape=[jax.ShapeDtypeStruct((B * T, ATT_WIDTH), BF16),
                   jax.ShapeDtypeStruct((B * BLOCK, KV_WIDTH), F32)],
        compiler_params=_params(("arbitrary", "arbitrary")),
        name="attention_prompt",
    )(u_att, u_att, u_att, u_att, u_att, ctab, stab, ctab, stab, jnp.tile(q_norm, reps)[None, :],
      jnp.tile(k_norm, reps)[None, :], sinks, _head_sum_matrix())


def _attn_sample_body(u_ref, k_ref, v_ref, ct_ref, st_ref, qn_ref, kn_ref, sink_ref, swap_ref, rexp_ref, bmask_ref,
                      o_ref, newk_ref, newv_ref, q_s, kx_s, vx_s, kn_s):
    b = pl.program_id(0)
    rexp = rexp_ref[...]

    @pl.when(b == 0)
    def _():
        swap = swap_ref[...]
        ct, st = ct_ref[...], st_ref[...]
        for h in range(N_Q_HEADS):
            sl = slice(h * HEAD_DIM, (h + 1) * HEAD_DIM)
            q_s[:, sl] = _head_norm_rope(u_ref[:, sl], qn_ref[...], ct, st, swap)
        for g in range(N_KV_HEADS):
            sl = slice(g * HEAD_DIM, (g + 1) * HEAD_DIM)
            kn_s[:, sl] = _head_norm_rope(u_ref[:, ATT_WIDTH + g * HEAD_DIM:ATT_WIDTH + (g + 1) * HEAD_DIM],
                                          kn_ref[...], ct, st, swap)
        kx_s[...] = _dot_exact_rhs(kn_s[...], rexp)
        vx_s[...] = _dot_exact_rhs(u_ref[:, ATT_WIDTH + KV_WIDTH:], rexp)

    bmask = bmask_ref[...]
    qfull = q_s[pl.ds(b, 1), :] * bmask
    kb, vb = k_ref[...], v_ref[...]
    kexp = _dot_exact_rhs(kb, rexp)
    vexp = _dot_exact_rhs(vb, rexp)
    s = lax.dot_general(qfull, kexp, (((1,), (1,)), ((), ())), preferred_element_type=F32,
                        precision=lax.Precision.HIGHEST) * ATT_SCALE
    s_self = jnp.sum(qfull * kx_s[pl.ds(b, 1), :], axis=-1, keepdims=True) * ATT_SCALE
    sink = sink_ref[...]
    m = jnp.maximum(jnp.maximum(jnp.max(s, axis=-1, keepdims=True), s_self), sink)
    p = jnp.exp(s - m)
    p_self = jnp.exp(s_self - m)
    denom = jnp.sum(p, axis=-1, keepdims=True) + p_self + jnp.exp(sink - m)
    o = _hp_dot(p / denom, vexp)
    o = o + (p_self / denom) * vx_s[pl.ds(b, 1), :]
    o_ref[pl.ds(b, 1), :] = jnp.sum(o * bmask, axis=0, keepdims=True)
    last = lax.broadcasted_iota(I32, kb.shape, 0) == WINDOW - 1
    newk_ref[...] = jnp.where(last, kn_s[pl.ds(b, 1), :], pltpu.roll(kb, WINDOW - 1, axis=0))
    newv_ref[...] = jnp.where(last, u_ref[pl.ds(b, 1), ATT_WIDTH + KV_WIDTH:], pltpu.roll(vb, WINDOW - 1, axis=0))


def attention_sample(u_att, row0, Bs, cache_k, cache_v, layer, q_norm, k_norm, sinks):
    assert row0 % Bs == 0
    ctab, stab, swap = _rope_tables(jnp.full((1,), PAST_LEN, I32))
    gd = jnp.arange(KV_WIDTH)
    hd = jnp.arange(ATT_WIDTH)
    rexp = ((gd[:, None] // HEAD_DIM == hd[None, :] // (HEAD_DIM * Q_PER_KV))
            & (gd[:, None] % HEAD_DIM == hd[None, :] % HEAD_DIM)).astype(BF16)
    bmask = (jnp.arange(N_Q_HEADS)[:, None] == hd[None, :] // HEAD_DIM).astype(F32)
    full = lambda shape: pl.BlockSpec(shape, lambda b: (0,) * len(shape))
    cspec = pl.BlockSpec((None, None, WINDOW, KV_WIDTH), lambda b: (layer, b, 0, 0))
    ospec = pl.BlockSpec((None, WINDOW, KV_WIDTH), lambda b: (b, 0, 0))
    return pl.pallas_call(
        _attn_sample_body,
        grid=(Bs,),
        in_specs=[pl.BlockSpec((Bs, ATT_PROJ), lambda b: (row0 // Bs, 0)), cspec, cspec,
                  full((1, HEAD_DIM)), full((1, HEAD_DIM)), full((1, HEAD_DIM)), full((1, HEAD_DIM)),
                  full((N_Q_HEADS, 1)), full((HEAD_DIM, HEAD_DIM)), full((KV_WIDTH, ATT_WIDTH)),
                  full((N_Q_HEADS, ATT_WIDTH))],
        out_specs=[full((Bs, ATT_WIDTH)), ospec, ospec],
        out_shape=[jax.ShapeDtypeStruct((Bs, ATT_WIDTH), F32),
                   jax.ShapeDtypeStruct((Bs, WINDOW, KV_WIDTH), F32),
                   jax.ShapeDtypeStruct((Bs, WINDOW, KV_WIDTH), F32)],
        scratch_shapes=[pltpu.VMEM((Bs, ATT_WIDTH), F32), pltpu.VMEM((Bs, ATT_WIDTH), F32),
                        pltpu.VMEM((Bs, ATT_WIDTH), F32), pltpu.VMEM((Bs, KV_WIDTH), F32)],
        compiler_params=_params(("arbitrary",)),
        name="attention_sample",
    )(u_att, cache_k, cache_v, ctab, stab, q_norm[None, :], k_norm[None, :], sinks[:, None], swap, rexp, bmask)


def _head_sum_matrix():
    i = jnp.arange(GROUP)
    return (i[:, None] // RWKV_HEAD == i[None, :] // RWKV_HEAD).astype(BF16)


def _head_sum(x, gmat):
    return jnp.concatenate([_dot_exact_rhs(x[:, q * GROUP:(q + 1) * GROUP], gmat) for q in range(N_GROUPS)], axis=1)


def _rwkv_prep_body(*refs, shifted, has_vres):
    it = iter(refs)
    u_ref = next(it)
    prev_ref = next(it)
    mu_ref, w0_ref, a0_ref, kk_ref, ka_ref, rk_ref, lora_ref, g_ref = (next(it) for _ in range(8))
    if has_vres:
        v0_ref, v1_ref, v2_ref, vf_ref = (next(it) for _ in range(4))
    r_o, w_o, k_o, v_o, a_o, b_o, g_o, bonus_o, tail_s = it
    u = u_ref[...]
    if shifted:
        first = jnp.where(pl.program_id(1) > 0, prev_ref[SUBLANES - 1:SUBLANES, :], 0.0)
        prev = jnp.where(lax.broadcasted_iota(I32, u.shape, 0) == 0, first, pltpu.roll(u, 1, axis=0))
    else:
        prev = prev_ref[...]
    xs = u + (prev - u) * mu_ref[...]
    r = xs[:, :RWKV_WIDTH]
    k = xs[:, RWKV_WIDTH:2 * RWKV_WIDTH]
    v = xs[:, 2 * RWKV_WIDTH:3 * RWKV_WIDTH]
    tail_s[...] = jnp.zeros(tail_s.shape, F32)
    tail_s[:, :LORA_IN] = xs[:, 3 * RWKV_WIDTH:]
    t = tail_s[...]
    def dot(x, w):
        if w.dtype == F32:
            return _hp_dot(x, w)
        return jnp.dot(x.astype(BF16), w, preferred_element_type=F32)

    dw = dot(jnp.tanh(t), lora_ref[0])
    da = dot(t, lora_ref[1])
    g = dot(jax.nn.sigmoid(t), lora_ref[2])
    z = -(w0_ref[...] + dw)
    softplus = jnp.maximum(z, 0.0) + jnp.log(1.0 + jnp.exp(-jnp.abs(z)))
    w_log = -softplus - 0.5
    decay = jnp.exp(-jnp.exp(w_log))
    a = jax.nn.sigmoid(a0_ref[...] + da)
    if has_vres:
        lo = dot(dot(v, v1_ref[...]), v2_ref[...])
        v = v + (vf_ref[...] - v) * jax.nn.sigmoid(v0_ref[...] + lo)
    gmat = g_ref[...]
    kk = k * kk_ref[...]
    kk = kk / jnp.maximum(jnp.sqrt(_head_sum(kk * kk, gmat)), 1e-12)
    k = k * (1.0 + (a - 1.0) * ka_ref[...])
    r_o[...] = r
    w_o[...] = decay
    k_o[...] = k
    v_o[...] = v
    a_o[...] = -kk
    b_o[...] = kk * a
    g_o[...] = g
    bonus_o[...] = _head_sum(r * k * rk_ref[...], gmat) * v


def rwkv_prep(u_rw, row0, B, T, prev_rows, p, v_first, tr):
    n = B * T
    shifted = prev_rows is None
    if shifted:
        assert T % tr == 0 and row0 == 0
        nt = T // tr
        grid = (B, nt)
        rowblk = lambda b, i: b * nt + i
        prev_spec = pl.BlockSpec((SUBLANES, SHIFT_WIDTH),
                                 lambda b, i: (jnp.maximum(rowblk(b, i) * (tr // SUBLANES) - 1, 0), 0))
        prev_arr = u_rw
        oblk = lambda b, i: (b * nt + i, 0)
    else:
        assert T == 1 and tr == B and row0 % tr == 0
        grid = (1, 1)
        rowblk = lambda b, i: row0 // tr
        prev_spec = pl.BlockSpec((tr, SHIFT_WIDTH), lambda b, i: (0, 0))
        prev_arr = prev_rows
        oblk = lambda b, i: (0, 0)
    vec = lambda w: pl.BlockSpec((1, w), lambda b, i: (0, 0))
    in_specs = [pl.BlockSpec((tr, SHIFT_WIDTH), lambda b, i: (rowblk(b, i), 0)), prev_spec,
                vec(SHIFT_WIDTH), vec(RWKV_WIDTH), vec(RWKV_WIDTH), vec(RWKV_WIDTH), vec(RWKV_WIDTH),
                vec(RWKV_WIDTH), pl.BlockSpec((3, LORA_PAD, RWKV_WIDTH), lambda b, i: (0, 0, 0)),
                pl.BlockSpec((GROUP, GROUP), lambda b, i: (0, 0))]
    args = [u_rw, prev_arr, p['mu'], p['w0'], p['a0'], p['k_k'], p['k_a'], p['r_k'], p['lora'], _head_sum_matrix()]
    has_vres = v_first is not None
    if has_vres:
        in_specs += [vec(RWKV_WIDTH), pl.BlockSpec((RWKV_WIDTH, MV_LORA), lambda b, i: (0, 0)),
                     pl.BlockSpec((MV_LORA, RWKV_WIDTH), lambda b, i: (0, 0)),
                     pl.BlockSpec((tr, RWKV_WIDTH), oblk)]
        args += [p['v0'], p['v1'], p['v2'], v_first]
    ospec = pl.BlockSpec((tr, RWKV_WIDTH), oblk)
    return pl.pallas_call(
        functools.partial(_rwkv_prep_body, shifted=shifted, has_vres=has_vres),
        grid=grid, in_specs=in_specs, out_specs=[ospec] * 8,
        out_shape=[jax.ShapeDtypeStruct((n, RWKV_WIDTH), F32)] * 8,
        scratch_shapes=[pltpu.VMEM((tr, LORA_PAD), F32)],
        compiler_params=_params(("arbitrary", "arbitrary")),
        name="rwkv_prep",
    )(*args)


def _rwkv_params(l, shift_mu, w0, w2, a0, a2, g2, k_k, k_a, r_k, v0, v1, v2):
    lora = jnp.zeros((3, LORA_PAD, RWKV_WIDTH), F32)
    lora = lora.at[0, :DECAY_LORA].set(w2[l])
    lora = lora.at[1, DECAY_LORA:DECAY_LORA + AAA_LORA].set(a2[l])
    lora = lora.at[2, DECAY_LORA + AAA_LORA:LORA_IN].set(g2[l])
    p = {'mu': shift_mu[l][None, :], 'w0': w0[l][None, :], 'a0': a0[l][None, :], 'k_k': k_k[l][None, :],
         'k_a': k_a[l][None, :], 'r_k': r_k[l].reshape(1, RWKV_WIDTH), 'lora': lora}
    if l > 0:
        p['v0'] = v0[l - 1][None, :]
        p['v1'] = v1[l - 1]
        p['v2'] = v2[l - 1]
    p_bf16 = {k_: (v_.astype(BF16) if k_ in ('lora', 'v1', 'v2') else v_) for k_, v_ in p.items()}
    return p, p_bf16


def _wkv_body(r_ref, w_ref, k_ref, v_ref, a_ref, b_ref, s0_ref, g_ref, eye_ref, y_ref, sout_ref, s_s, sb_s, *, nb,
              tc, precise):
    c = pl.program_id(1)

    @pl.when(c == 0)
    def _():
        s_s[...] = s0_ref[...]
        sb_s[...] = s0_ref[...].astype(BF16)

    gmat = g_ref[...]
    eye = eye_ref[...][None]
    eye_b = eye.astype(BF16)
    rows = nb * RWKV_HEAD
    dot = functools.partial(jnp.dot, preferred_element_type=F32)

    groups = [slice(q * GROUP, (q + 1) * GROUP) for q in range(N_GROUPS)]

    def block(red, n):
        return red[n * rows:(n + 1) * rows].reshape(nb, RWKV_HEAD, GROUP)

    def diagonal(yx):
        d = (yx * eye).reshape(nb, RWKV_HEAD // SUBLANES, SUBLANES, GROUP)
        return jnp.sum(jnp.sum(d, axis=1), axis=1, keepdims=True)

    def store_y(t, yx):
        for q, sl in enumerate(groups):
            y_ref[:, pl.ds(t, 1), sl] = diagonal(block(yx, q))

    def step_bf16(t, carry):
        tp = jnp.maximum(t - 1, 0)
        w_t, k_t, v_t, a_t, b_t = (ref[:, pl.ds(t, 1), :] for ref in (w_ref, k_ref, v_ref, a_ref, b_ref))
        a_b, v_b, r_b = a_t.astype(BF16), v_t.astype(BF16), r_ref[:, pl.ds(tp, 1), :].astype(BF16)
        tiles = (nb, RWKV_HEAD // BF16_ROWS, BF16_ROWS, GROUP)
        lhs = []
        for q, sl in enumerate(groups):
            sb = sb_s[q]
            parts = [(sb * a_b[:, :, sl]).reshape(tiles), (eye_b * v_b[:, :, sl]).reshape(tiles),
                     (sb * r_b[:, :, sl]).reshape(tiles)]
            lhs.append(jnp.stack(parts, axis=2).reshape(3 * rows, GROUP))
        red = dot(jnp.concatenate(lhs, axis=0), gmat)
        ys = []
        for q, sl in enumerate(groups):
            rq = red[3 * q * rows:3 * (q + 1) * rows].reshape(tiles[:2] + (3,) + tiles[2:])
            sa, vx = (rq[:, :, j].reshape(nb, RWKV_HEAD, GROUP) for j in range(2))
            s_new = s_s[q] * w_t[:, :, sl] + sa * b_t[:, :, sl] + vx * k_t[:, :, sl]
            s_s[q] = s_new
            sb_s[q] = s_new.astype(BF16)
            ys.append(diagonal(rq[:, :, 2].reshape(nb, RWKV_HEAD, GROUP)))

        @pl.when(t > 0)
        def _():
            for q, sl in enumerate(groups):
                y_ref[:, pl.ds(tp, 1), sl] = ys[q]

        return carry

    def last_y_bf16():
        r_b = r_ref[:, pl.ds(tc - 1, 1), :].astype(BF16)
        lhs = [(sb_s[q] * r_b[:, :, sl]).reshape(rows, GROUP) for q, sl in enumerate(groups)]
        store_y(tc - 1, dot(jnp.concatenate(lhs, axis=0), gmat))

    def head_sums3(xs):
        lhs = []
        for x in xs:
            lhs += list(_split3(x.reshape(rows, GROUP)))
        red = dot(jnp.concatenate(lhs, axis=0), gmat)
        out = []
        for q in range(len(xs)):
            o = 3 * q * rows
            acc = red[o:o + rows] + red[o + rows:o + 2 * rows] + red[o + 2 * rows:o + 3 * rows]
            out.append(acc.reshape(nb, RWKV_HEAD, GROUP))
        return out

    def step_f32(t, carry):
        r_t, w_t, k_t, v_t, a_t, b_t = (ref[:, pl.ds(t, 1), :] for ref in (r_ref, w_ref, k_ref, v_ref, a_ref, b_ref))
        groups = [slice(q * GROUP, (q + 1) * GROUP) for q in range(N_GROUPS)]
        sa = head_sums3([s_s[q] * a_t[:, :, sl] for q, sl in enumerate(groups)])
        vx = head_sums3([eye * v_t[:, :, sl] for sl in groups])
        prods = []
        for q, sl in enumerate(groups):
            s_new = s_s[q] * w_t[:, :, sl] + sa[q] * b_t[:, :, sl] + vx[q] * k_t[:, :, sl]
            s_s[q] = s_new
            prods.append(s_new * r_t[:, :, sl])
        yx = head_sums3(prods)
        for q, sl in enumerate(groups):
            y_ref[:, pl.ds(t, 1), sl] = jnp.sum(yx[q] * eye, axis=1, keepdims=True)
        return carry

    if precise:
        lax.fori_loop(0, tc, step_f32, 0)
    else:
        lax.fori_loop(0, tc, step_bf16, 0)
        last_y_bf16()

    @pl.when(c == pl.num_programs(1) - 1)
    def _():
        sout_ref[...] = s_s[...]


def wkv_scan(r, w, k, v, a, b, s0, nb=4, tc=64, precise=False):
    B, T, W = r.shape
    tc = min(tc, T)
    assert B % nb == 0 and T % tc == 0
    xspec = pl.BlockSpec((nb, tc, W), lambda i, c: (i, c, 0))
    sspec = pl.BlockSpec((N_GROUPS, nb, RWKV_HEAD, GROUP), lambda i, c: (0, i, 0, 0))
    j = jnp.arange(GROUP)
    eye = (jnp.arange(RWKV_HEAD)[:, None] == j[None, :] % RWKV_HEAD).astype(F32)
    return pl.pallas_call(
        functools.partial(_wkv_body, nb=nb, tc=tc, precise=precise),
        grid=(B // nb, T // tc),
        in_specs=[xspec] * 6 + [sspec, pl.BlockSpec((GROUP, GROUP), lambda i, c: (0, 0)),
                                pl.BlockSpec((RWKV_HEAD, GROUP), lambda i, c: (0, 0))],
        out_specs=[xspec, sspec],
        out_shape=[jax.ShapeDtypeStruct((B, T, W), F32), jax.ShapeDtypeStruct(s0.shape, F32)],
        scratch_shapes=[pltpu.VMEM((N_GROUPS, nb, RWKV_HEAD, GROUP), F32),
                        pltpu.VMEM((N_GROUPS, nb, RWKV_HEAD, GROUP), BF16)],
        compiler_params=_params(("arbitrary", "arbitrary")),
        name="wkv_scan",
    )(r, w, k, v, a, b, s0, _head_sum_matrix(), eye)


def _state_to_groups(s):
    B = s.shape[0]
    return s.reshape(B, N_GROUPS, HEADS_PER_GROUP, RWKV_HEAD, RWKV_HEAD).transpose(1, 0, 3, 2, 4).reshape(
        N_GROUPS, B, RWKV_HEAD, GROUP)


def _state_from_groups(s):
    B = s.shape[1]
    return s.reshape(N_GROUPS, B, RWKV_HEAD, HEADS_PER_GROUP, RWKV_HEAD).transpose(1, 0, 3, 2, 4).reshape(
        B, N_RWKV_HEADS, RWKV_HEAD, RWKV_HEAD)


def _rwkv_post_body(y_ref, bonus_ref, g_ref, lw_ref, lb_ref, gm_ref, o_ref):
    gmat = gm_ref[...]
    y = y_ref[...]
    mean = _head_sum(y, gmat) * (1.0 / RWKV_HEAD)
    d = y - mean
    var = _head_sum(d * d, gmat) * (1.0 / RWKV_HEAD)
    yn = d * lax.rsqrt(var + GN_EPS) * lw_ref[...] + lb_ref[...]
    o_ref[...] = ((yn + bonus_ref[...]) * g_ref[...]).astype(o_ref.dtype)


def rwkv_post(y, bonus, g, ln_w, ln_b, out_dtype, tr=256):
    n, W = y.shape
    tr = min(tr, n)
    assert n % tr == 0
    spec = pl.BlockSpec((tr, W), lambda i: (i, 0))
    vec = pl.BlockSpec((1, W), lambda i: (0, 0))
    return pl.pallas_call(
        _rwkv_post_body, grid=(n // tr,),
        in_specs=[spec, spec, spec, vec, vec, pl.BlockSpec((GROUP, GROUP), lambda i: (0, 0))],
        out_specs=spec, out_shape=jax.ShapeDtypeStruct((n, W), out_dtype),
        compiler_params=_params(("parallel",)),
        name="rwkv_post",
    )(y, bonus, g, ln_w[None, :], ln_b[None, :], _head_sum_matrix())


def _rwkv_mixer(u_rw, B, T, prev_rows, s0_groups, p, v_first, ln_w, ln_b, tr, precise):
    r, w, k, v, a, b, g, bonus = rwkv_prep(u_rw, 0, B, T, prev_rows, p, v_first, tr)
    sh = lambda t: t.reshape(B, T, RWKV_WIDTH)
    y, s_out = wkv_scan(sh(r), sh(w), sh(k), sh(v), sh(a), sh(b), s0_groups, precise=precise)
    out = rwkv_post(y.reshape(B * T, RWKV_WIDTH), bonus, g, ln_w, ln_b, F32 if precise else BF16)
    return out, _state_from_groups(s_out), v


TM = 1024
TM_MOE = 512
TC = 256
TR_PREP = 128
TN_SMALL = 512


def kernel(x_prompt, x_sample, state_shift, state_wkv, cache_win_k, cache_win_v, attn_norm_w, w_in, q_norm_w,
           k_norm_w, attn_sinks, shift_mu, w0, w2, a0, a2, g2, k_k, k_a, r_k, ln_x_w, ln_x_b, v0, v1, v2, w_out,
           ffn_norm_w, dense_w1, dense_w3, dense_w2, router_w, moe_w1, moe_w3, moe_w2):
    Bp, Tp, D = x_prompt.shape
    Bs, Ts, _ = x_sample.shape
    assert Ts == 1 and cache_win_k.shape[2] == WINDOW and Tp % BLOCK == 0
    Mp, Ms = Bp * Tp, Bs * Ts
    assert Mp % TM == 0 and Mp % TC == 0
    xp = x_prompt.reshape(Mp, D)
    xs = x_sample.reshape(Ms, D)
    cache_k = cache_win_k.reshape(DEPTH, Bs, WINDOW, KV_WIDTH)
    cache_v = cache_win_v.reshape(DEPTH, Bs, WINDOW, KV_WIDTH)
    n_moe_tiles = (TOP_K * (Mp + Ms)) // TM_MOE + N_EXPERTS
    zero_state = jnp.zeros((N_GROUPS, Bp, RWKV_HEAD, GROUP), F32)
    vf_p = vf_s = None
    outs = {k_: [] for k_ in ('p_row', 'p_wkv', 'p_k', 'p_v', 's_row', 's_wkv', 's_k', 's_v')}
    for l in range(DEPTH):
        te, mode = _dense_tiles(Mp, TM, l)
        p_f32, p_bf16 = _rwkv_params(l, shift_mu, w0, w2, a0, a2, g2, k_k, k_a, r_k, v0, v1, v2)
        an = attn_norm_w[l][None, :]
        h = rmsnorm_bf16(xp, an)
        u_att = matmul_proj(h, w_in, te, mode, n_lo=0, n_cols=ATT_PROJ, tm=TM, tn=512)
        u_rw = matmul_proj(h, w_in, te, mode, n_lo=ATT_PROJ, n_cols=SHIFT_WIDTH, tm=TM, tn=512)
        att_p, knew_p = attention_prompt(u_att, Bp, Tp, q_norm_w[l], k_norm_w[l], attn_sinks[l])
        rw_p, S_p, v_p_first = _rwkv_mixer(u_rw, Bp, Tp, None, zero_state, p_bf16, vf_p, ln_x_w[l], ln_x_b[l],
                                           TR_PREP, False)
        xp = matmul_residual(jnp.concatenate([att_p, rw_p], axis=1), w_out, xp, te, mode, tm=TM, tn=512, tk=D,
                             in_place=l > 0)
        us_att = small_matmul('proj', xs, [w_in], l, tn=TN_SMALL, norm_w=an, n_lo=0, n_cols=ATT_PROJ)
        us_rw = small_matmul('proj', xs, [w_in], l, tn=TN_SMALL, norm_w=an, n_lo=ATT_PROJ, n_cols=SHIFT_WIDTH)
        att_s, k_s, v_s = attention_sample(us_att, 0, Bs, cache_k, cache_v, l, q_norm_w[l], k_norm_w[l],
                                           attn_sinks[l])
        rw_s, S_s, v_s_first = _rwkv_mixer(us_rw, Bs, Ts, state_shift[l], _state_to_groups(state_wkv[l]), p_f32,
                                           vf_s, ln_x_w[l], ln_x_b[l], Bs, True)
        xs = small_matmul('res', jnp.concatenate([att_s, rw_s], axis=1), [w_out], l, tn=TN_SMALL, res=xs)
        if l == 0:
            vf_p, vf_s = v_p_first, v_s_first
        fn = ffn_norm_w[l][None, :]
        if l % 2 == 0:
            te, mode = _dense_tiles(Mp, TM, l // 2)
            h = rmsnorm_bf16(xp, fn)
            g = matmul_swiglu_in(h, dense_w1, dense_w3, te, mode, tm=TM, tn=512)
            xp = matmul_residual(g, dense_w2, xp, te, mode, tm=TM, tn=1024, tk=D_FF // 4)
            gs = small_matmul('swiglu', xs, [dense_w1, dense_w3], l // 2, tn=TN_SMALL // 2, norm_w=fn)
            xs = small_matmul('res', gs, [dense_w2], l // 2, tn=TN_SMALL // 4, res=xs)
        else:
            router_pad = jnp.pad(router_w[l // 2], ((0, 0), (0, LANES - N_EXPERTS)))
            idx_p, gates_p = route_top2(xp, fn, router_pad)
            idx_s, gates_s = route_top2(xs, fn, router_pad, tr=Ms)
            te, mode, src, gate_rows, slot = _route_tables(
                jnp.concatenate([idx_p, idx_s], axis=0), jnp.concatenate([gates_p, gates_s], axis=0),
                TM_MOE, n_moe_tiles)
            rows = gather_rmsnorm_bf16(xp, xs, fn, src, tg=TC)
            g = matmul_swiglu_in(rows, moe_w1[l // 2], moe_w3[l // 2], te, mode, tm=TM_MOE, tn=512, parts=True)
            ys = matmul_residual(g, moe_w2[l // 2], None, te, mode,
                                 tm=TM_MOE, tn=1024, tk=D_FF // 4, parts=True, scale=gate_rows)
            xp = combine_top2(xp, ys, slot[:Mp, 0], slot[:Mp, 1], tc=TC)
            xs = combine_top2(xs, ys, slot[Mp:, 0], slot[Mp:, 1], tc=Ms)
        u_att_p = u_att.reshape(Bp, Tp, ATT_PROJ)
        outs['p_row'].append(u_rw.reshape(Bp, Tp, SHIFT_WIDTH)[:, Tp - 1])
        outs['p_wkv'].append(S_p)
        outs['p_k'].append(knew_p.reshape(Bp, WINDOW, N_KV_HEADS, HEAD_DIM))
        outs['p_v'].append(u_att_p[:, Tp - WINDOW:, ATT_WIDTH + KV_WIDTH:].reshape(Bp, WINDOW, N_KV_HEADS, HEAD_DIM))
        outs['s_row'].append(us_rw)
        outs['s_wkv'].append(S_s)
        outs['s_k'].append(k_s.reshape(Bs, WINDOW, N_KV_HEADS, HEAD_DIM))
        outs['s_v'].append(v_s.reshape(Bs, WINDOW, N_KV_HEADS, HEAD_DIM))
    return (xp.reshape(Bp, Tp, D), xs.reshape(Bs, Ts, D),
            jnp.stack(outs['p_row']), jnp.stack(outs['p_wkv']), jnp.stack(outs['p_k']), jnp.stack(outs['p_v']),
            jnp.stack(outs['s_row']), jnp.stack(outs['s_wkv']), jnp.stack(outs['s_k']), jnp.stack(outs['s_v']))
```
